```python
import math
import jax, jax.numpy as jnp
from jax import lax
import numpy as np

D_MODEL = 1024
BATCH = 32
SEQ = 2048
DEPTH = 2

N_EVEN = (DEPTH + 1) // 2
N_ODD = DEPTH // 2

D_FF = 2816
FFN_RES = 0.5
NORM_EPS = 1e-6

POOL_WINDOWS = (2, 4, 8, 16)
N_POOL_GROUPS = 4
POOL_WIDTH = D_MODEL // 2
POOL_GROUP = POOL_WIDTH // N_POOL_GROUPS

CONV_WIDTH = D_MODEL // 2
CONV_KERNEL = 31

AB_IN = POOL_WIDTH + 2 * CONV_WIDTH
AB_OUT = POOL_WIDTH + CONV_WIDTH

N_HEADS = 16
HEAD_DIM = 64
ATTN_WIDTH = N_HEADS * HEAD_DIM
N_IDX_HEADS = 8
IDX_DIM = 64
TOPK_MAX = 256
Q_BLOCK = 128
C_SPLITS = (ATTN_WIDTH, ATTN_WIDTH, ATTN_WIDTH, N_IDX_HEADS * IDX_DIM, IDX_DIM, N_IDX_HEADS)
C_IN = sum(C_SPLITS)

REL_BUCKETS = 32
REL_MAX_DIST = 128

kernel_name = 'hybrid_pool_conv_dsa_macaron'


def _offsets(sizes):
    out, acc = [], 0
    for s in sizes[:-1]:
        acc += s
        out.append(acc)
    return out


def rms_norm(x, g):
    xf = x.astype(jnp.float32)
    y = xf * lax.rsqrt(jnp.mean(xf * xf, axis=-1, keepdims=True) + NORM_EPS)
    return (y * g.astype(jnp.float32)).astype(x.dtype)


def swiglu(h, w_gate, w_up, w_down):
    return (jax.nn.silu(h @ w_gate) * (h @ w_up)) @ w_down


def pool_mixer(u, pool_w, pool_scale):
    B, S, _ = u.shape
    uf = u.astype(jnp.float32).reshape(B, S, N_POOL_GROUPS, POOL_GROUP)
    c0 = jnp.concatenate([jnp.zeros((B, 1, N_POOL_GROUPS, POOL_GROUP), jnp.float32),
                          jnp.cumsum(uf, axis=1)], axis=1)
    t = jnp.arange(S, dtype=jnp.int32)[:, None]
    win = jnp.array(POOL_WINDOWS, dtype=jnp.int32)[None, :]
    lo = jnp.maximum(t + 1 - win, 0)
    g_idx = jnp.arange(N_POOL_GROUPS, dtype=jnp.int32)[None, :]
    lo_sum = c0[:, lo, g_idx, :]
    count = (t + 1 - lo).astype(jnp.float32)
    mean = (c0[:, 1:] - lo_sum) / count[None, :, :, None]
    y = jnp.einsum('bsgc,gcd->bsgd', mean - uf, pool_w.astype(jnp.float32))
    y = y * pool_scale.astype(jnp.float32).reshape(N_POOL_GROUPS, POOL_GROUP)
    return y.reshape(B, S, POOL_WIDTH).astype(u.dtype)


def conv_module(a, gate, conv_w, conv_b, ln_g, ln_b):
    z = a * jax.nn.sigmoid(gate)
    z = lax.conv_general_dilated(
        z, conv_w[:, None, :].astype(z.dtype), window_strides=(1,),
        padding=[(CONV_KERNEL - 1, 0)], dimension_numbers=('NWC', 'WIO', 'NWC'),
        feature_group_count=CONV_WIDTH) + conv_b
    zf = z.astype(jnp.float32)
    mu = jnp.mean(zf, axis=-1, keepdims=True)
    var = jnp.mean(jnp.square(zf - mu), axis=-1, keepdims=True)
    zf = (zf - mu) * lax.rsqrt(var + NORM_EPS) * ln_g.astype(jnp.float32) + ln_b.astype(jnp.float32)
    return jax.nn.silu(zf).astype(a.dtype)


def t5_bucket(dist):
    max_exact = REL_BUCKETS // 2
    d = jnp.maximum(dist, max_exact).astype(jnp.float32)
    large = max_exact + (jnp.log(d / max_exact) / math.log(REL_MAX_DIST / max_exact)
                         * (REL_BUCKETS - max_exact)).astype(jnp.int32)
    large = jnp.minimum(large, REL_BUCKETS - 1)
    return jnp.where(dist < max_exact, dist, large)


def dsa_attention(q, k, v, qi, ki, wi, rel_bias):
    S = q.shape[1]
    top_k = min(TOPK_MAX, S // 4)
    nb = S // Q_BLOCK
    pos = jnp.arange(S, dtype=jnp.int32)
    idx_scale = (N_IDX_HEADS * IDX_DIM) ** -0.5
    attn_scale = HEAD_DIM ** -0.5

    def one_seq(args):
        qb, kb, vb, qib, kib, wib = args
        kib32 = kib.astype(jnp.float32)

        def one_block(bargs):
            q_t, qi_t, wi_t, t = bargs
            rel = jax.nn.relu(jnp.einsum('thd,sd->ths', qi_t.astype(jnp.float32), kib32))
            score = jnp.einsum('ths,th->ts', rel, wi_t.astype(jnp.float32)) * idx_scale
            score = jnp.where(pos[None, :] <= t[:, None], score, -jnp.inf)
            _, sel = lax.top_k(score, top_k)
            k_sel = kb[sel]
            v_sel = vb[sel]
            logits = jnp.einsum('thd,tkhd->thk', q_t, k_sel).astype(jnp.float32) * attn_scale
            bias = rel_bias[t5_bucket(t[:, None] - sel)]
            logits = logits + jnp.transpose(bias, (0, 2, 1)).astype(jnp.float32)
            logits = jnp.where((sel <= t[:, None])[:, None, :], logits, -jnp.inf)
            p = jax.nn.softmax(logits, axis=-1).astype(v_sel.dtype)
            return jnp.einsum('thk,tkhd->thd', p, v_sel)

        blocks = (qb.reshape(nb, Q_BLOCK, N_HEADS, HEAD_DIM),
                  qib.reshape(nb, Q_BLOCK, N_IDX_HEADS, IDX_DIM),
                  wib.reshape(nb, Q_BLOCK, N_IDX_HEADS),
                  pos.reshape(nb, Q_BLOCK))
        out = lax.map(one_block, blocks)
        return out.reshape(S, ATTN_WIDTH)

    return lax.map(one_seq, (q, k, v, qi, ki, wi))


def setup_inputs(seed: int = 0) -> dict:
    key = jax.random.key(seed)
    ks = jax.random.split(key, 24)
    f32 = jnp.float32

    def w(k, shape, fan_in):
        return jax.random.normal(k, shape, f32) * (fan_in ** -0.5)

    def gain(k, shape):
        return 1.0 + 0.02 * jax.random.normal(k, shape, f32)

    def small(k, shape):
        return 0.01 * jax.random.normal(k, shape, f32)

    return {
        'x': jax.random.normal(ks[0], (BATCH, SEQ, D_MODEL), f32),
        'ffn1_norm': gain(ks[1], (DEPTH, D_MODEL)),
        'ffn1_w_gate': w(ks[2], (DEPTH, D_MODEL, D_FF), D_MODEL),
        'ffn1_w_up': w(ks[3], (DEPTH, D_MODEL, D_FF), D_MODEL),
        'ffn1_w_down': w(ks[4], (DEPTH, D_FF, D_MODEL), D_FF),
        'mix_norm': gain(ks[5], (DEPTH, D_MODEL)),
        'ffn2_norm': gain(ks[6], (DEPTH, D_MODEL)),
        'ffn2_w_gate': w(ks[7], (DEPTH, D_MODEL, D_FF), D_MODEL),
        'ffn2_w_up': w(ks[8], (DEPTH, D_MODEL, D_FF), D_MODEL),
        'ffn2_w_down': w(ks[9], (DEPTH, D_FF, D_MODEL), D_FF),
        'ab_w_in': w(ks[10], (N_EVEN, D_MODEL, AB_IN), D_MODEL),
        'pool_w': w(ks[11], (N_EVEN, N_POOL_GROUPS, POOL_GROUP, POOL_GROUP), POOL_GROUP),
        'pool_scale': gain(ks[12], (N_EVEN, POOL_WIDTH)),
        'conv_w': w(ks[13], (N_EVEN, CONV_KERNEL, CONV_WIDTH), CONV_KERNEL),
        'conv_b': small(ks[14], (N_EVEN, CONV_WIDTH)),
        'conv_ln_g': gain(ks[15], (N_EVEN, CONV_WIDTH)),
        'conv_ln_b': small(ks[16], (N_EVEN, CONV_WIDTH)),
        'ab_w_out': w(ks[17], (N_EVEN, AB_OUT, D_MODEL), AB_OUT),
        'c_w_in': w(ks[18], (N_ODD, D_MODEL, C_IN), D_MODEL),
        'c_w_out': w(ks[19], (N_ODD, ATTN_WIDTH, D_MODEL), ATTN_WIDTH),
        'rel_bias': 0.1 * jax.random.normal(ks[20], (REL_BUCKETS, N_HEADS), f32),
        'final_norm': gain(ks[21], (D_MODEL,)),
    }


def reference(x, ffn1_norm, ffn1_w_gate, ffn1_w_up, ffn1_w_down, mix_norm, ffn2_norm,
              ffn2_w_gate, ffn2_w_up, ffn2_w_down, ab_w_in, pool_w, pool_scale, conv_w,
              conv_b, conv_ln_g, conv_ln_b, ab_w_out, c_w_in, c_w_out, rel_bias, final_norm):
    B, S, _ = x.shape
    ab_cuts = _offsets((POOL_WIDTH, CONV_WIDTH, CONV_WIDTH))
    c_cuts = _offsets(C_SPLITS)
    for layer in range(DEPTH):
        h = rms_norm(x, ffn1_norm[layer])
        x = x + FFN_RES * swiglu(h, ffn1_w_gate[layer], ffn1_w_up[layer], ffn1_w_down[layer])
        h = rms_norm(x, mix_norm[layer])
        i = layer // 2
        if layer % 2 == 0:
            proj = h @ ab_w_in[i]
            u_pool, u_val, u_gate = jnp.split(proj, ab_cuts, axis=-1)
            y_pool = pool_mixer(u_pool, pool_w[i], pool_scale[i])
            y_conv = conv_module(u_val, u_gate, conv_w[i], conv_b[i], conv_ln_g[i], conv_ln_b[i])
            y = jnp.concatenate([y_pool, y_conv], axis=-1) @ ab_w_out[i]
        else:
            proj = h @ c_w_in[i]
            q, k, v, qi, ki, wi = jnp.split(proj, c_cuts, axis=-1)
            attn = dsa_attention(q.reshape(B, S, N_HEADS, HEAD_DIM),
                                 k.reshape(B, S, N_HEADS, HEAD_DIM),
                                 v.reshape(B, S, N_HEADS, HEAD_DIM),
                                 qi.reshape(B, S, N_IDX_HEADS, IDX_DIM), ki, wi, rel_bias)
            y = attn @ c_w_out[i]
        x = x + y
        h = rms_norm(x, ffn2_norm[layer])
        x = x + FFN_RES * swiglu(h, ffn2_w_gate[layer], ffn2_w_up[layer], ffn2_w_down[layer])
    return rms_norm(x, final_norm)
```

```python
import functools
import math

import jax
import jax.numpy as jnp
from jax import lax
from jax.experimental import pallas as pl
from jax.experimental.pallas import tpu as pltpu

F32 = jnp.float32
I32 = jnp.int32
_MXU_DTYPE = jnp.bfloat16

NORM_EPS = 1e-6
FFN_RES = 0.5
POOL_WINDOWS = (2, 4, 8, 16)
POOL_GROUP = 128
CONV_KERNEL = 31
N_HEADS = 16
HEAD_DIM = 64
N_IDX_HEADS = 8
IDX_DIM = 64
TOPK_MAX = 256
REL_BUCKETS = 32
REL_MAX_DIST = 128

V7X_VMEM_BYTES = 64 * 1024 * 1024
VMEM_LIMIT = V7X_VMEM_BYTES - 6 * 1024 * 1024
SUBLANES = 8
LANES = 128

FFN_TOKENS = 512
MIX_TOKENS = 512
HALO = 32
TQ = 256
KC = 256
INT_MIN = -2 ** 31


def _rms(x, g):
    return x * lax.rsqrt(jnp.mean(x * x, axis=-1, keepdims=True) + NORM_EPS) * g


def _mm(a, b):
    return jnp.dot(a, b, preferred_element_type=F32)


def _mm_nt(a, b):
    return lax.dot_general(a, b, (((1,), (1,)), ((), ())), preferred_element_type=F32)


def _mm_tn(a, b):
    return lax.dot_general(a, b, (((0,), (0,)), ((), ())), preferred_element_type=F32)


def _params():
    return pltpu.CompilerParams(vmem_limit_bytes=VMEM_LIMIT)


def _full(shape):
    n = len(shape)
    return pl.BlockSpec(shape, lambda *_: (0,) * n)


def _ffn_kernel(x_ref, g_ref, wg_ref, wu_ref, wd_ref, fg_ref, o_ref, *, final_norm):
    x = x_ref[...]
    h = _rms(x, g_ref[...]).astype(_MXU_DTYPE)
    gate = _mm(h, wg_ref[...])
    up = _mm(h, wu_ref[...])
    act = (gate * jax.nn.sigmoid(gate) * up).astype(_MXU_DTYPE)
    y = x + FFN_RES * _mm(act, wd_ref[...])
    if final_norm:
        y = _rms(y, fg_ref[...])
    o_ref[...] = y


def _ffn(x2, g, wg, wu, wd, fg, final_norm):
    n, d = x2.shape
    f = wg.shape[1]
    t = FFN_TOKENS
    return pl.pallas_call(
        functools.partial(_ffn_kernel, final_norm=final_norm),
        grid=(n // t,),
        in_specs=[
            pl.BlockSpec((t, d), lambda i: (i, 0)),
            _full((1, d)), _full((d, f)), _full((d, f)), _full((f, d)), _full((1, d)),
        ],
        out_specs=pl.BlockSpec((t, d), lambda i: (i, 0)),
        out_shape=jax.ShapeDtypeStruct((n, d), F32),
        compiler_params=_params(),
        name="ffn_final" if final_norm else "ffn",
    )(x2, g, wg, wu, wd, fg)


def _ab_kernel(x_ref, g_ref, win_ref, pw_ref, ps_ref, cw_ref, cb_ref, lg_ref, lb_ref, wout_ref,
               o_ref, ubuf, zbuf):
    t = x_ref.shape[1]
    pw_width = ps_ref.shape[1]
    cw_width = cb_ref.shape[1]
    j = pl.program_id(1)

    @pl.when(j == 0)
    def _():
        ubuf[0:HALO, :] = jnp.zeros((HALO, pw_width), F32)
        zbuf[0:HALO, :] = jnp.zeros((HALO, cw_width), F32)

    @pl.when(j > 0)
    def _():
        ubuf[0:HALO, :] = ubuf[t:t + HALO, :]
        zbuf[0:HALO, :] = zbuf[t:t + HALO, :]

    x = x_ref[0]
    h = _rms(x, g_ref[...]).astype(_MXU_DTYPE)
    proj = _mm(h, win_ref[...])
    u = proj[:, 0:pw_width]
    val = proj[:, pw_width:pw_width + cw_width]
    gate = proj[:, pw_width + cw_width:pw_width + 2 * cw_width]
    ubuf[HALO:HALO + t, :] = u
    zbuf[HALO:HALO + t, :] = val * jax.nn.sigmoid(gate)

    pos = j * t + lax.broadcasted_iota(I32, (t, POOL_GROUP), 0)
    ys = []
    for g, w in enumerate(POOL_WINDOWS):
        lo, hi = g * POOL_GROUP, (g + 1) * POOL_GROUP
        ug = ubuf[HALO:HALO + t, lo:hi]
        acc = ug
        for back in range(1, w):
            acc = acc + ubuf[HALO - back:HALO - back + t, lo:hi]
        count = jnp.minimum(pos + 1, w).astype(F32)
        diff = (acc / count - ug).astype(_MXU_DTYPE)
        ys.append(_mm(diff, pw_ref[g]) * ps_ref[:, lo:hi])

    conv = jnp.broadcast_to(cb_ref[...], (t, cw_width))
    for k in range(CONV_KERNEL):
        off = HALO - (CONV_KERNEL - 1) + k
        conv = conv + zbuf[off:off + t, :] * cw_ref[k:k + 1, :]
    mu = jnp.mean(conv, axis=-1, keepdims=True)
    cen = conv - mu
    var = jnp.mean(cen * cen, axis=-1, keepdims=True)
    zn = cen * lax.rsqrt(var + NORM_EPS) * lg_ref[...] + lb_ref[...]
    ys.append(zn * jax.nn.sigmoid(zn))

    ycat = jnp.concatenate(ys, axis=-1).astype(_MXU_DTYPE)
    o_ref[0] = x + _mm(ycat, wout_ref[...])


def _ab_mixer(x, g, win, pw, ps, cw, cb, lg, lb, wout):
    b, s, d = x.shape
    t = MIX_TOKENS
    pwid, cwid = ps.shape[1], cb.shape[1]
    return pl.pallas_call(
        _ab_kernel,
        grid=(b, s // t),
        in_specs=[
            pl.BlockSpec((1, t, d), lambda bi, j: (bi, j, 0)),
            _full((1, d)), _full(win.shape), _full(pw.shape), _full(ps.shape), _full(cw.shape),
            _full(cb.shape), _full(lg.shape), _full(lb.shape), _full(wout.shape),
        ],
        out_specs=pl.BlockSpec((1, t, d), lambda bi, j: (bi, j, 0)),
        out_shape=jax.ShapeDtypeStruct((b, s, d), F32),
        scratch_shapes=[pltpu.VMEM((HALO + t, pwid), F32), pltpu.VMEM((HALO + t, cwid), F32)],
        compiler_params=_params(),
        name="ab_mixer",
    )(x, g, win, pw, ps, cw, cb, lg, lb, wout)


def _cin_kernel(x_ref, g_ref, wn_ref, wt_ref, ww_ref, k_out, ki_out, qt_out, vt_out, qit_out, wt_out):
    t = x_ref.shape[1]
    aw = qt_out.shape[1]
    iw = qit_out.shape[1]
    h = _rms(x_ref[0], g_ref[...]).astype(_MXU_DTYPE)
    nat = _mm(h, wn_ref[...])
    for p in range(aw // LANES):
        k_out[0, p] = nat[:, p * LANES:(p + 1) * LANES].astype(k_out.dtype)
    ki_out[0] = nat[:, aw:aw + IDX_DIM].astype(ki_out.dtype)
    tr = _mm_nt(wt_ref[...], h)
    qt_out[0] = (tr[0:aw] * (HEAD_DIM ** -0.5)).astype(qt_out.dtype)
    for c in range(t // KC):
        vt_out[0, c] = tr[aw:2 * aw, c * KC:(c + 1) * KC].astype(vt_out.dtype)
    qit_out[0] = tr[2 * aw:2 * aw + iw].astype(qit_out.dtype)
    wt_out[0] = _mm_nt(ww_ref[...], h)


def _c_inproj(x, g, wn, wt, ww):
    b, s, d = x.shape
    t = MIX_TOKENS
    aw = N_HEADS * HEAD_DIM
    iw = N_IDX_HEADS * IDX_DIM
    md = _MXU_DTYPE
    return pl.pallas_call(
        _cin_kernel,
        grid=(b, s // t),
        in_specs=[
            pl.BlockSpec((1, t, d), lambda bi, j: (bi, j, 0)),
            _full((1, d)), _full(wn.shape), _full(wt.shape), _full(ww.shape),
        ],
        out_specs=[
            pl.BlockSpec((1, aw // LANES, t, LANES), lambda bi, j: (bi, 0, j, 0)),
            pl.BlockSpec((1, t, IDX_DIM), lambda bi, j: (bi, j, 0)),
            pl.BlockSpec((1, aw, t), lambda bi, j: (bi, 0, j)),
            pl.BlockSpec((1, t // KC, aw, KC), lambda bi, j: (bi, j, 0, 0)),
            pl.BlockSpec((1, iw, t), lambda bi, j: (bi, 0, j)),
            pl.BlockSpec((1, N_IDX_HEADS, t), lambda bi, j: (bi, 0, j)),
        ],
        out_shape=[
            jax.ShapeDtypeStruct((b, aw // LANES, s, LANES), md),
            jax.ShapeDtypeStruct((b, s, IDX_DIM), md),
            jax.ShapeDtypeStruct((b, aw, s), md),
            jax.ShapeDtypeStruct((b, s // KC, aw, KC), md),
            jax.ShapeDtypeStruct((b, iw, s), md),
            jax.ShapeDtypeStruct((b, N_IDX_HEADS, s), F32),
        ],
        compiler_params=_params(),
        name="c_inproj",
    )(x, g, wn, wt, ww)


def _bias_kernel(rel_ref, bkt_ref, o_ref):
    h = pl.program_id(1)
    bkt = bkt_ref[0]
    acc = jnp.zeros(bkt.shape, F32)
    for b in range(REL_BUCKETS):
        acc = jnp.where(bkt == b, rel_ref[b, h], acc)
    o_ref[0, 0] = acc


def _bias_tiles(rel_bias, bkt):
    nd = bkt.shape[0]
    return pl.pallas_call(
        _bias_kernel,
        grid=(nd, N_HEADS),
        in_specs=[
            pl.BlockSpec(memory_space=pltpu.SMEM),
            pl.BlockSpec((1, KC, TQ), lambda dl, h: (dl, 0, 0)),
        ],
        out_specs=pl.BlockSpec((1, 1, KC, TQ), lambda dl, h: (dl, h, 0, 0)),
        out_shape=jax.ShapeDtypeStruct((nd, N_HEADS, KC, TQ), F32),
        name="bias_tiles",
    )(rel_bias, bkt)


def _t5_bucket(dist):
    max_exact = REL_BUCKETS // 2
    d = jnp.maximum(dist, max_exact).astype(F32)
    large = max_exact + (jnp.log(d / max_exact) / math.log(REL_MAX_DIST / max_exact)
                         * (REL_BUCKETS - max_exact)).astype(I32)
    large = jnp.minimum(large, REL_BUCKETS - 1)
    return jnp.where(dist < max_exact, dist, large)


def _colsum8(v):
    return v.reshape(v.shape[0] // SUBLANES, SUBLANES, v.shape[1]).sum(axis=0)


def _colmax8(v):
    return v.reshape(v.shape[0] // SUBLANES, SUBLANES, v.shape[1]).max(axis=0)


def _dsa_kernel(far_ref, x_ref, qt_ref, qit_ref, wt_ref, k_ref, ki_ref, vt_ref, tile_ref, wout_ref,
                o_ref, key_scr, madd_scr, logit_scr, ot_scr, m_scr, *, top_k):
    i = pl.program_id(1)
    nk = i + 1
    idx_scale = (N_IDX_HEADS * IDX_DIM) ** -0.5
    row = lax.broadcasted_iota(I32, (KC, TQ), 0)
    col = lax.broadcasted_iota(I32, (KC, TQ), 1)
    qpos = i * TQ + col

    def chunk(c):
        return pl.ds(pl.multiple_of(c * KC, KC), KC)

    def score_body(c, carry):
        ki_c = ki_ref[0, chunk(c), :]
        acc = jnp.zeros((KC, TQ), F32)
        for h in range(N_IDX_HEADS):
            r = _mm(ki_c, qit_ref[0, h * IDX_DIM:(h + 1) * IDX_DIM, :])
            acc = acc + jnp.maximum(r, 0.0) * wt_ref[0, h:h + 1, :]
        bits = lax.bitcast_convert_type(acc * idx_scale, I32)
        key = jnp.where(bits < 0, bits ^ 0x7FFFFFFF, bits)
        key_scr[chunk(c), :] = jnp.where(row + c * KC <= qpos, key, INT_MIN)
        return carry

    lax.fori_loop(0, nk, score_body, 0)

    def count(pred, thr):
        def body(c, acc):
            return acc + _colsum8(pred(key_scr[chunk(c), :], thr).astype(I32))
        acc = lax.fori_loop(0, nk, body, jnp.zeros((SUBLANES, TQ), I32))
        return acc.sum(axis=0, keepdims=True)

    def bit_body(it, t_u):
        cand = t_u | lax.shift_left(jnp.int32(1), 31 - it)
        cnt = count(lambda k, thr: k >= thr, cand ^ INT_MIN)
        return jnp.where(cnt >= top_k, cand, t_u)

    thr = lax.fori_loop(0, 32, bit_body, jnp.zeros((1, TQ), I32)) ^ INT_MIN

    need = (top_k - count(lambda k, t_: k > t_, thr)).astype(F32)
    tri = (lax.broadcasted_iota(I32, (KC, KC), 1) < lax.broadcasted_iota(I32, (KC, KC), 0)).astype(_MXU_DTYPE)

    def sel_body(c, seen):
        k = key_scr[chunk(c), :]
        eq = (k == thr).astype(F32)
        before = _mm(tri, eq.astype(_MXU_DTYPE)) + seen
        sel = (k > thr) | ((eq > 0.0) & (before < need))
        sel = sel & (k != INT_MIN)
        madd_scr[chunk(c), :] = jnp.where(sel, 0.0, -jnp.inf)
        return seen + _colsum8(eq).sum(axis=0, keepdims=True)

    lax.fori_loop(0, nk, sel_body, jnp.zeros((1, TQ), F32))

    pair_row = lax.broadcasted_iota(I32, (2 * HEAD_DIM, TQ), 0)

    def head_body(h, carry):
        p = h // 2
        lo = (h % 2) * HEAD_DIM
        qt = qt_ref[0, pl.ds(pl.multiple_of(p * 2 * HEAD_DIM, 2 * HEAD_DIM), 2 * HEAD_DIM), :]
        qt = jnp.where((pair_row >= lo) & (pair_row < lo + HEAD_DIM), qt, jnp.zeros_like(qt))

        def logits(c, bias):
            s = _mm(k_ref[0, p, chunk(c), :], qt) + bias + madd_scr[chunk(c), :]
            logit_scr[chunk(c), :] = s
            return _colmax8(s)

        far = far_ref[h]
        m_scr[...] = lax.fori_loop(0, jnp.maximum(nk - 2, 0),
                                   lambda c, m: jnp.maximum(m, logits(c, far)),
                                   jnp.full((SUBLANES, TQ), -jnp.inf, F32))

        @pl.when(nk >= 2)
        def _():
            m_scr[...] = jnp.maximum(m_scr[...], logits(nk - 2, tile_ref[1, h]))

        m = jnp.maximum(m_scr[...], logits(nk - 1, tile_ref[0, h])).max(axis=0, keepdims=True)

        def pv_body(c, lc):
            l, acc = lc
            e = jnp.exp(logit_scr[chunk(c), :] - m)
            vt = vt_ref[0, c, pl.ds(pl.multiple_of(h * HEAD_DIM, HEAD_DIM), HEAD_DIM), :]
            return l + _colsum8(e), acc + _mm(vt, e.astype(_MXU_DTYPE))

        l, acc = lax.fori_loop(0, nk, pv_body,
                               (jnp.zeros((SUBLANES, TQ), F32), jnp.zeros((HEAD_DIM, TQ), F32)))
        ot_scr[pl.ds(pl.multiple_of(h * HEAD_DIM, HEAD_DIM), HEAD_DIM), :] = acc / l.sum(axis=0, keepdims=True)
        return carry

    lax.fori_loop(0, N_HEADS, head_body, 0)

    o_ref[0] = x_ref[0] + _mm_tn(ot_scr[...].astype(_MXU_DTYPE), wout_ref[...])


def _dsa(x, far, qt, qit, wt, k, ki, vt, tiles, wout, top_k):
    b, s, d = x.shape
    aw = N_HEADS * HEAD_DIM
    iw = N_IDX_HEADS * IDX_DIM
    return pl.pallas_call(
        functools.partial(_dsa_kernel, top_k=top_k),
        grid=(b, s // TQ),
        in_specs=[
            pl.BlockSpec(memory_space=pltpu.SMEM),
            pl.BlockSpec((1, TQ, d), lambda bi, i: (bi, i, 0)),
            pl.BlockSpec((1, aw, TQ), lambda bi, i: (bi, 0, i)),
            pl.BlockSpec((1, iw, TQ), lambda bi, i: (bi, 0, i)),
            pl.BlockSpec((1, N_IDX_HEADS, TQ), lambda bi, i: (bi, 0, i)),
            pl.BlockSpec((1, aw // LANES, s, LANES), lambda bi, i: (bi, 0, 0, 0)),
            pl.BlockSpec((1, s, IDX_DIM), lambda bi, i: (bi, 0, 0)),
            pl.BlockSpec((1, s // KC, aw, KC), lambda bi, i: (bi, 0, 0, 0)),
            _full(tiles.shape), _full(wout.shape),
        ],
        out_specs=pl.BlockSpec((1, TQ, d), lambda bi, i: (bi, i, 0)),
        out_shape=jax.ShapeDtypeStruct((b, s, d), F32),
        scratch_shapes=[
            pltpu.VMEM((s, TQ), I32),
            pltpu.VMEM((s, TQ), F32),
            pltpu.VMEM((s, TQ), F32),
            pltpu.VMEM((aw, TQ), F32),
            pltpu.VMEM((SUBLANES, TQ), F32),
        ],
        compiler_params=_params(),
        name="dsa",
    )(far, x, qt, qit, wt, k, ki, vt, tiles, wout)


def kernel(x, ffn1_norm, ffn1_w_gate, ffn1_w_up, ffn1_w_down, mix_norm, ffn2_norm, ffn2_w_gate,
           ffn2_w_up, ffn2_w_down, ab_w_in, pool_w, pool_scale, conv_w, conv_b, conv_ln_g,
           conv_ln_b, ab_w_out, c_w_in, c_w_out, rel_bias, final_norm):
    b, s, d = x.shape
    depth = ffn1_norm.shape[0]
    md = _MXU_DTYPE
    aw = N_HEADS * HEAD_DIM
    iw = N_IDX_HEADS * IDX_DIM
    top_k = min(TOPK_MAX, s // 4)
    assert s % MIX_TOKENS == 0 and (b * s) % FFN_TOKENS == 0 and s % TQ == 0 and top_k <= KC

    row = lambda v: v.reshape(1, -1)

    if depth > 1:
        sk = jnp.arange(KC, dtype=I32)[:, None]
        tq = jnp.arange(TQ, dtype=I32)[None, :]
        bkt = jnp.stack([_t5_bucket(jnp.maximum(tq + dl * TQ - sk, 0)) for dl in (0, 1)])
        tiles = _bias_tiles(rel_bias, bkt)
        assert 2 * TQ - KC + 1 >= REL_MAX_DIST
        far = rel_bias[REL_BUCKETS - 1]

    for layer in range(depth):
        i = layer // 2
        last = layer == depth - 1
        x2 = _ffn(x.reshape(b * s, d), row(ffn1_norm[layer]), ffn1_w_gate[layer].astype(md),
                  ffn1_w_up[layer].astype(md), ffn1_w_down[layer].astype(md), row(final_norm), False)
        x = x2.reshape(b, s, d)
        if layer % 2 == 0:
            x = _ab_mixer(x, row(mix_norm[layer]), ab_w_in[i].astype(md), pool_w[i].astype(md),
                          row(pool_scale[i]), conv_w[i], row(conv_b[i]), row(conv_ln_g[i]),
                          row(conv_ln_b[i]), ab_w_out[i].astype(md))
        else:
            w = c_w_in[i]
            wq, wk, wv = w[:, 0:aw], w[:, aw:2 * aw], w[:, 2 * aw:3 * aw]
            wqi = w[:, 3 * aw:3 * aw + iw]
            wki = w[:, 3 * aw + iw:3 * aw + iw + IDX_DIM]
            www = w[:, 3 * aw + iw + IDX_DIM:]
            wn = jnp.concatenate([wk, wki], axis=1).astype(md)
            wt = jnp.concatenate([wq, wv, wqi], axis=1).T.astype(md)
            k, ki, qt, vt, qit, wts = _c_inproj(x, row(mix_norm[layer]), wn, wt, www.T.astype(md))
            x = _dsa(x, far, qt, qit, wts, k, ki, vt, tiles, c_w_out[i].astype(md), top_k)
        x2 = _ffn(x.reshape(b * s, d), row(ffn2_norm[layer]), ffn2_w_gate[layer].astype(md),
                  ffn2_w_up[layer].astype(md), ffn2_w_down[layer].astype(md), row(final_norm), last)
        x = x2.reshape(b, s, d)
    return x
```

```python
import functools
import math

import jax
import jax.numpy as jnp
from jax import lax
from jax.experimental import pallas as pl
from jax.experimental.pallas import tpu as pltpu

F32 = jnp.float32
I32 = jnp.int32
_MXU_DTYPE = jnp.bfloat16

NORM_EPS = 1e-6
FFN_RES = 0.5
POOL_WINDOWS = (2, 4, 8, 16)
POOL_GROUP = 128
CONV_KERNEL = 31
N_HEADS = 16
HEAD_DIM = 64
N_IDX_HEADS = 8
IDX_DIM = 64
TOPK_MAX = 256
REL_BUCKETS = 32
REL_MAX_DIST = 128

V7X_VMEM_BYTES = 64 * 1024 * 1024
VMEM_LIMIT = V7X_VMEM_BYTES - 6 * 1024 * 1024
SUBLANES = 8
LANES = 128

FFN_TOKENS = 512
MIX_TOKENS = 512
HALO = 32
TQ = 256
KC = 256
INT_MIN = -2 ** 31


def _rms(x, g):
    return x * lax.rsqrt(jnp.mean(x * x, axis=-1, keepdims=True) + NORM_EPS) * g


def _mm(a, b):
    return jnp.dot(a, b, preferred_element_type=F32)


def _mm_nt(a, b):
    return lax.dot_general(a, b, (((1,), (1,)), ((), ())), preferred_element_type=F32)


def _mm_tn(a, b):
    return lax.dot_general(a, b, (((0,), (0,)), ((), ())), preferred_element_type=F32)


def _params():
    return pltpu.CompilerParams(vmem_limit_bytes=VMEM_LIMIT)


def _full(shape):
    n = len(shape)
    return pl.BlockSpec(shape, lambda *_: (0,) * n)


def _ffn_kernel(x_ref, g_ref, wg_ref, wu_ref, wd_ref, fg_ref, o_ref, *, final_norm):
    x = x_ref[...]
    h = _rms(x, g_ref[...]).astype(_MXU_DTYPE)
    gate = _mm(h, wg_ref[...])
    up = _mm(h, wu_ref[...])
    act = (gate * jax.nn.sigmoid(gate) * up).astype(_MXU_DTYPE)
    y = x + FFN_RES * _mm(act, wd_ref[...])
    if final_norm:
        y = _rms(y, fg_ref[...])
    o_ref[...] = y


def _ffn(x2, g, wg, wu, wd, fg, final_norm):
    n, d = x2.shape
    f = wg.shape[1]
    t = FFN_TOKENS
    return pl.pallas_call(
        functools.partial(_ffn_kernel, final_norm=final_norm),
        grid=(n // t,),
        in_specs=[
            pl.BlockSpec((t, d), lambda i: (i, 0)),
            _full((1, d)), _full((d, f)), _full((d, f)), _full((f, d)), _full((1, d)),
        ],
        out_specs=pl.BlockSpec((t, d), lambda i: (i, 0)),
        out_shape=jax.ShapeDtypeStruct((n, d), F32),
        compiler_params=_params(),
        name="ffn_final" if final_norm else "ffn",
    )(x2, g, wg, wu, wd, fg)


def _ab_kernel(x_ref, g_ref, win_ref, pw_ref, ps_ref, cw_ref, cb_ref, lg_ref, lb_ref, wout_ref,
               o_ref, ubuf, zbuf):
    t = x_ref.shape[1]
    pw_width = ps_ref.shape[1]
    cw_width = cb_ref.shape[1]
    j = pl.program_id(1)

    @pl.when(j == 0)
    def _():
        ubuf[0:HALO, :] = jnp.zeros((HALO, pw_width), F32)
        zbuf[0:HALO, :] = jnp.zeros((HALO, cw_width), F32)

    @pl.when(j > 0)
    def _():
        ubuf[0:HALO, :] = ubuf[t:t + HALO, :]
        zbuf[0:HALO, :] = zbuf[t:t + HALO, :]

    x = x_ref[0]
    h = _rms(x, g_ref[...]).astype(_MXU_DTYPE)
    proj = _mm(h, win_ref[...])
    u = proj[:, 0:pw_width]
    val = proj[:, pw_width:pw_width + cw_width]
    gate = proj[:, pw_width + cw_width:pw_width + 2 * cw_width]
    ubuf[HALO:HALO + t, :] = u
    zbuf[HALO:HALO + t, :] = val * jax.nn.sigmoid(gate)

    pos = j * t + lax.broadcasted_iota(I32, (t, POOL_GROUP), 0)
    ys = []
    for g, w in enumerate(POOL_WINDOWS):
        lo, hi = g * POOL_GROUP, (g + 1) * POOL_GROUP
        ug = ubuf[HALO:HALO + t, lo:hi]
        acc = ug
        for back in range(1, w):
            acc = acc + ubuf[HALO - back:HALO - back + t, lo:hi]
        count = jnp.minimum(pos + 1, w).astype(F32)
        diff = (acc / count - ug).astype(_MXU_DTYPE)
        ys.append(_mm(diff, pw_ref[g]) * ps_ref[:, lo:hi])

    conv = jnp.broadcast_to(cb_ref[...], (t, cw_width))
    for k in range(CONV_KERNEL):
        off = HALO - (CONV_KERNEL - 1) + k
        conv = conv + zbuf[off:off + t, :] * cw_ref[k:k + 1, :]
    mu = jnp.mean(conv, axis=-1, keepdims=True)
    cen = conv - mu
    var = jnp.mean(cen * cen, axis=-1, keepdims=True)
    zn = cen * lax.rsqrt(var + NORM_EPS) * lg_ref[...] + lb_ref[...]
    ys.append(zn * jax.nn.sigmoid(zn))

    ycat = jnp.concatenate(ys, axis=-1).astype(_MXU_DTYPE)
    o_ref[0] = x + _mm(ycat, wout_ref[...])


def _ab_mixer(x, g, win, pw, ps, cw, cb, lg, lb, wout):
    b, s, d = x.shape
    t = MIX_TOKENS
    pwid, cwid = ps.shape[1], cb.shape[1]
    return pl.pallas_call(
        _ab_kernel,
        grid=(b, s // t),
        in_specs=[
            pl.BlockSpec((1, t, d), lambda bi, j: (bi, j, 0)),
            _full((1, d)), _full(win.shape), _full(pw.shape), _full(ps.shape), _full(cw.shape),
            _full(cb.shape), _full(lg.shape), _full(lb.shape), _full(wout.shape),
        ],
        out_specs=pl.BlockSpec((1, t, d), lambda bi, j: (bi, j, 0)),
        out_shape=jax.ShapeDtypeStruct((b, s, d), F32),
        scratch_shapes=[pltpu.VMEM((HALO + t, pwid), F32), pltpu.VMEM((HALO + t, cwid), F32)],
        compiler_params=_params(),
        name="ab_mixer",
    )(x, g, win, pw, ps, cw, cb, lg, lb, wout)


def _cin_kernel(x_ref, g_ref, wn_ref, wt_ref, ww_ref, k_out, ki_out, qt_out, vt_out, qit_out, wt_out):
    t = x_ref.shape[1]
    aw = qt_out.shape[1]
    iw = qit_out.shape[1]
    h = _rms(x_ref[0], g_ref[...]).astype(_MXU_DTYPE)
    nat = _mm(h, wn_ref[...])
    for p in range(aw // LANES):
        k_out[0, p] = nat[:, p * LANES:(p + 1) * LANES].astype(k_out.dtype)
    ki_out[0] = nat[:, aw:aw + IDX_DIM].astype(ki_out.dtype)
    tr = _mm_nt(wt_ref[...], h)
    qt_out[0] = (tr[0:aw] * (HEAD_DIM ** -0.5)).astype(qt_out.dtype)
    for c in range(t // KC):
        vt_out[0, c] = tr[aw:2 * aw, c * KC:(c + 1) * KC].astype(vt_out.dtype)
    qit_out[0] = tr[2 * aw:2 * aw + iw].astype(qit_out.dtype)
    wt_out[0] = _mm_nt(ww_ref[...], h)


def _c_inproj(x, g, wn, wt, ww):
    b, s, d = x.shape
    t = MIX_TOKENS
    aw = N_HEADS * HEAD_DIM
    iw = N_IDX_HEADS * IDX_DIM
    md = _MXU_DTYPE
    return pl.pallas_call(
        _cin_kernel,
        grid=(b, s // t),
        in_specs=[
            pl.BlockSpec((1, t, d), lambda bi, j: (bi, j, 0)),
            _full((1, d)), _full(wn.shape), _full(wt.shape), _full(ww.shape),
        ],
        out_specs=[
            pl.BlockSpec((1, aw // LANES, t, LANES), lambda bi, j: (bi, 0, j, 0)),
            pl.BlockSpec((1, t, IDX_DIM), lambda bi, j: (bi, j, 0)),
            pl.BlockSpec((1, aw, t), lambda bi, j: (bi, 0, j)),
            pl.BlockSpec((1, t // KC, aw, KC), lambda bi, j: (bi, j, 0, 0)),
            pl.BlockSpec((1, iw, t), lambda bi, j: (bi, 0, j)),
            pl.BlockSpec((1, N_IDX_HEADS, t), lambda bi, j: (bi, 0, j)),
        ],
        out_shape=[
            jax.ShapeDtypeStruct((b, aw // LANES, s, LANES), md),
            jax.ShapeDtypeStruct((b, s, IDX_DIM), md),
            jax.ShapeDtypeStruct((b, aw, s), md),
            jax.ShapeDtypeStruct((b, s // KC, aw, KC), md),
            jax.ShapeDtypeStruct((b, iw, s), md),
            jax.ShapeDtypeStruct((b, N_IDX_HEADS, s), F32),
        ],
        compiler_params=_params(),
        name="c_inproj",
    )(x, g, wn, wt, ww)


def _bias_kernel(rel_ref, bkt_ref, o_ref):
    h = pl.program_id(1)
    bkt = bkt_ref[0]
    acc = jnp.zeros(bkt.shape, F32)
    for b in range(REL_BUCKETS):
        acc = jnp.where(bkt == b, rel_ref[b, h], acc)
    o_ref[0, 0] = acc


def _bias_tiles(rel_bias, bkt):
    nd = bkt.shape[0]
    return pl.pallas_call(
        _bias_kernel,
        grid=(nd, N_HEADS),
        in_specs=[
            pl.BlockSpec(memory_space=pltpu.SMEM),
            pl.BlockSpec((1, KC, TQ), lambda dl, h: (dl, 0, 0)),
        ],
        out_specs=pl.BlockSpec((1, 1, KC, TQ), lambda dl, h: (dl, h, 0, 0)),
        out_shape=jax.ShapeDtypeStruct((nd, N_HEADS, KC, TQ), F32),
        name="bias_tiles",
    )(rel_bias, bkt)


def _t5_bucket(dist):
    max_exact = REL_BUCKETS // 2
    d = jnp.maximum(dist, max_exact).astype(F32)
    large = max_exact + (jnp.log(d / max_exact) / math.log(REL_MAX_DIST / max_exact)
                         * (REL_BUCKETS - max_exact)).astype(I32)
    large = jnp.minimum(large, REL_BUCKETS - 1)
    return jnp.where(dist < max_exact, dist, large)


def _colsum8(v):
    return v.reshape(v.shape[0] // SUBLANES, SUBLANES, v.shape[1]).sum(axis=0)


def _colmax8(v):
    return v.reshape(v.shape[0] // SUBLANES, SUBLANES, v.shape[1]).max(axis=0)


def _dsa_block(n, far_ref, qt_ref, qit_ref, wt_ref, k_ref, ki_ref, vt_ref, tile_ref,
               key_scr, madd_scr, logit_scr, ot_scr, *, top_k):
    idx_scale = (N_IDX_HEADS * IDX_DIM) ** -0.5
    causal = lax.broadcasted_iota(I32, (KC, TQ), 0) <= lax.broadcasted_iota(I32, (KC, TQ), 1)
    rows = lambda c: slice(c * KC, (c + 1) * KC)

    for c in range(n):
        ki_c = ki_ref[0, rows(c), :]
        acc = jnp.zeros((KC, TQ), F32)
        for h in range(N_IDX_HEADS):
            r = _mm(ki_c, qit_ref[0, h * IDX_DIM:(h + 1) * IDX_DIM, :])
            acc = acc + jnp.maximum(r, 0.0) * wt_ref[0, h:h + 1, :]
        bits = lax.bitcast_convert_type(acc * idx_scale, I32)
        key = jnp.where(bits < 0, bits ^ 0x7FFFFFFF, bits)
        if c == n - 1:
            key = jnp.where(causal, key, INT_MIN)
        key_scr[rows(c), :] = key

    if n * KC <= top_k:
        for c in range(n):
            madd_scr[rows(c), :] = jnp.where(key_scr[rows(c), :] != INT_MIN, 0.0, -jnp.inf)
    else:
        def count(pred, thr):
            acc = jnp.zeros((SUBLANES, TQ), I32)
            for c in range(n):
                acc = acc + _colsum8(pred(key_scr[rows(c), :], thr).astype(I32))
            return acc.sum(axis=0, keepdims=True)

        def bit_body(it, t_u):
            cand = t_u | lax.shift_left(jnp.int32(1), 31 - it)
            cnt = count(lambda k, t_: k >= t_, cand ^ INT_MIN)
            return jnp.where(cnt >= top_k, cand, t_u)

        thr = lax.fori_loop(0, 32, bit_body, jnp.zeros((1, TQ), I32)) ^ INT_MIN

        need = (top_k - count(lambda k, t_: k > t_, thr)).astype(F32)
        tri = (lax.broadcasted_iota(I32, (KC, KC), 1)
               < lax.broadcasted_iota(I32, (KC, KC), 0)).astype(_MXU_DTYPE)
        seen = jnp.zeros((1, TQ), F32)
        for c in range(n):
            k = key_scr[rows(c), :]
            eq = (k == thr).astype(F32)
            before = _mm(tri, eq.astype(_MXU_DTYPE)) + seen
            sel = ((k > thr) | ((eq > 0.0) & (before < need))) & (k != INT_MIN)
            madd_scr[rows(c), :] = jnp.where(sel, 0.0, -jnp.inf)
            seen = seen + _colsum8(eq).sum(axis=0, keepdims=True)

    pair_row = lax.broadcasted_iota(I32, (2 * HEAD_DIM, TQ), 0)
    n_far = max(n - 2, 0)

    def pair_body(p, carry):
        qt = qt_ref[0, pl.ds(pl.multiple_of(p * 2 * HEAD_DIM, 2 * HEAD_DIM), 2 * HEAD_DIM), :]
        for half in range(2):
            h = 2 * p + half
            far = far_ref[h]
            q0 = jnp.where((pair_row >= half * HEAD_DIM) & (pair_row < (half + 1) * HEAD_DIM),
                           qt, jnp.zeros_like(qt))
            m_far = jnp.full((SUBLANES, TQ), -jnp.inf, F32)
            m_near = m_far
            for c in range(n):
                s = _mm(k_ref[0, p, rows(c), :], q0) + madd_scr[rows(c), :]
                if c < n_far:
                    m_far = jnp.maximum(m_far, _colmax8(s))
                else:
                    s = s + tile_ref[n - 1 - c, h]
                    m_near = jnp.maximum(m_near, _colmax8(s))
                logit_scr[half, rows(c), :] = s
            m = jnp.maximum(m_far + far, m_near).max(axis=0, keepdims=True)
            m_minus_far = m - far
            l = jnp.zeros((SUBLANES, TQ), F32)
            acc = jnp.zeros((HEAD_DIM, TQ), F32)
            hrow = pl.ds(pl.multiple_of(h * HEAD_DIM, HEAD_DIM), HEAD_DIM)
            for c in range(n):
                e = jnp.exp(logit_scr[half, rows(c), :] - (m_minus_far if c < n_far else m))
                l = l + _colsum8(e)
                acc = acc + _mm(vt_ref[0, c, hrow, :], e.astype(_MXU_DTYPE))
            ot_scr[hrow, :] = acc / l.sum(axis=0, keepdims=True)
        return carry

    lax.fori_loop(0, N_HEADS // 2, pair_body, 0)


def _dsa_kernel(far_ref, x_ref, qt_ref, qit_ref, wt_ref, k_ref, ki_ref, vt_ref, tile_ref, wout_ref,
                o_ref, key_scr, madd_scr, logit_scr, ot_scr, *, top_k):
    i = pl.program_id(1)
    for n in range(1, k_ref.shape[2] // KC + 1):
        pl.when(i == n - 1)(functools.partial(
            _dsa_block, n, far_ref, qt_ref, qit_ref, wt_ref, k_ref, ki_ref, vt_ref, tile_ref,
            key_scr, madd_scr, logit_scr, ot_scr, top_k=top_k))
    o_ref[0] = x_ref[0] + _mm_tn(ot_scr[...].astype(_MXU_DTYPE), wout_ref[...])


def _dsa(x, far, qt, qit, wt, k, ki, vt, tiles, wout, top_k):
    b, s, d = x.shape
    aw = N_HEADS * HEAD_DIM
    iw = N_IDX_HEADS * IDX_DIM
    return pl.pallas_call(
        functools.partial(_dsa_kernel, top_k=top_k),
        grid=(b, s // TQ),
        in_specs=[
            pl.BlockSpec(memory_space=pltpu.SMEM),
            pl.BlockSpec((1, TQ, d), lambda bi, i: (bi, i, 0)),
            pl.BlockSpec((1, aw, TQ), lambda bi, i: (bi, 0, i)),
            pl.BlockSpec((1, iw, TQ), lambda bi, i: (bi, 0, i)),
            pl.BlockSpec((1, N_IDX_HEADS, TQ), lambda bi, i: (bi, 0, i)),
            pl.BlockSpec((1, aw // LANES, s, LANES), lambda bi, i: (bi, 0, 0, 0)),
            pl.BlockSpec((1, s, IDX_DIM), lambda bi, i: (bi, 0, 0)),
            pl.BlockSpec((1, s // KC, aw, KC), lambda bi, i: (bi, 0, 0, 0)),
            _full(tiles.shape), _full(wout.shape),
        ],
        out_specs=pl.BlockSpec((1, TQ, d), lambda bi, i: (bi, i, 0)),
        out_shape=jax.ShapeDtypeStruct((b, s, d), F32),
        scratch_shapes=[
            pltpu.VMEM((s, TQ), I32),
            pltpu.VMEM((s, TQ), F32),
            pltpu.VMEM((2, s, TQ), F32),
            pltpu.VMEM((aw, TQ), F32),
        ],
        compiler_params=_params(),
        name="dsa",
    )(far, x, qt, qit, wt, k, ki, vt, tiles, wout)


def kernel(x, ffn1_norm, ffn1_w_gate, ffn1_w_up, ffn1_w_down, mix_norm, ffn2_norm, ffn2_w_gate,
           ffn2_w_up, ffn2_w_down, ab_w_in, pool_w, pool_scale, conv_w, conv_b, conv_ln_g,
           conv_ln_b, ab_w_out, c_w_in, c_w_out, rel_bias, final_norm):
    b, s, d = x.shape
    depth = ffn1_norm.shape[0]
    md = _MXU_DTYPE
    aw = N_HEADS * HEAD_DIM
    iw = N_IDX_HEADS * IDX_DIM
    top_k = min(TOPK_MAX, s // 4)
    assert s % MIX_TOKENS == 0 and (b * s) % FFN_TOKENS == 0 and s % TQ == 0 and top_k <= KC

    row = lambda v: v.reshape(1, -1)

    if depth > 1:
        sk = jnp.arange(KC, dtype=I32)[:, None]
        tq = jnp.arange(TQ, dtype=I32)[None, :]
        bkt = jnp.stack([_t5_bucket(jnp.maximum(tq + dl * TQ - sk, 0)) for dl in (0, 1)])
        tiles = _bias_tiles(rel_bias, bkt)
        assert 2 * TQ - KC + 1 >= REL_MAX_DIST
        far = rel_bias[REL_BUCKETS - 1]

    for layer in range(depth):
        i = layer // 2
        last = layer == depth - 1
        x2 = _ffn(x.reshape(b * s, d), row(ffn1_norm[layer]), ffn1_w_gate[layer].astype(md),
                  ffn1_w_up[layer].astype(md), ffn1_w_down[layer].astype(md), row(final_norm), False)
        x = x2.reshape(b, s, d)
        if layer % 2 == 0:
            x = _ab_mixer(x, row(mix_norm[layer]), ab_w_in[i].astype(md), pool_w[i].astype(md),
                          row(pool_scale[i]), conv_w[i], row(conv_b[i]), row(conv_ln_g[i]),
                          row(conv_ln_b[i]), ab_w_out[i].astype(md))
        else:
            w = c_w_in[i]
            wq, wk, wv = w[:, 0:aw], w[:, aw:2 * aw], w[:, 2 * aw:3 * aw]
            wqi = w[:, 3 * aw:3 * aw + iw]
            wki = w[:, 3 * aw + iw:3 * aw + iw + IDX_DIM]
            www = w[:, 3 * aw + iw + IDX_DIM:]
            wn = jnp.concatenate([wk, wki], axis=1).astype(md)
            wt = jnp.concatenate([wq, wv, wqi], axis=1).T.astype(md)
            k, ki, qt, vt, qit, wts = _c_inproj(x, row(mix_norm[layer]), wn, wt, www.T.astype(md))
            x = _dsa(x, far, qt, qit, wts, k, ki, vt, tiles, c_w_out[i].astype(md), top_k)
        x2 = _ffn(x.reshape(b * s, d), row(ffn2_norm[layer]), ffn2_w_gate[layer].astype(md),
                  ffn2_w_up[layer].astype(md), ffn2_w_down[layer].astype(md), row(final_norm), last)
        x = x2.reshape(b, s, d)
    return x
```

```python
import functools
import math

import jax
import jax.numpy as jnp
from jax import lax
from jax.experimental import pallas as pl
from jax.experimental.pallas import tpu as pltpu

F32 = jnp.float32
I32 = jnp.int32
_MXU_DTYPE = jnp.bfloat16

NORM_EPS = 1e-6
FFN_RES = 0.5
POOL_WINDOWS = (2, 4, 8, 16)
POOL_GROUP = 128
CONV_KERNEL = 31
N_HEADS = 16
HEAD_DIM = 64
N_IDX_HEADS = 8
IDX_DIM = 64
TOPK_MAX = 256
REL_BUCKETS = 32
REL_MAX_DIST = 128

V7X_VMEM_BYTES = 64 * 1024 * 1024
VMEM_LIMIT = V7X_VMEM_BYTES - 6 * 1024 * 1024
SUBLANES = 8
LANES = 128

FFN_TOKENS = 512
MIX_TOKENS = 512
CONV_ROWS = 32
HALO = 32
TQ = 256
KC = 256
INT_MIN = -2 ** 31
LOG2E = 1.0 / math.log(2.0)
ONES_ROWS = 16


def _rms(x, g):
    return x * lax.rsqrt(jnp.mean(x * x, axis=-1, keepdims=True) + NORM_EPS) * g


def _mm(a, b):
    return jnp.dot(a, b, preferred_element_type=F32)


def _mm_nt(a, b):
    return lax.dot_general(a, b, (((1,), (1,)), ((), ())), preferred_element_type=F32)


def _mm_tn(a, b):
    return lax.dot_general(a, b, (((0,), (0,)), ((), ())), preferred_element_type=F32)


def _params():
    return pltpu.CompilerParams(vmem_limit_bytes=VMEM_LIMIT)


def _full(shape):
    n = len(shape)
    return pl.BlockSpec(shape, lambda *_: (0,) * n)


def _ffn_kernel(x_ref, g_ref, wg_ref, wu_ref, wd_ref, fg_ref, o_ref, *, final_norm):
    x = x_ref[...]
    h = _rms(x, g_ref[...]).astype(_MXU_DTYPE)
    gate = _mm(h, wg_ref[...])
    up = _mm(h, wu_ref[...])
    act = (gate * jax.nn.sigmoid(gate) * up).astype(_MXU_DTYPE)
    y = x + FFN_RES * _mm(act, wd_ref[...])
    if final_norm:
        y = _rms(y, fg_ref[...])
    o_ref[...] = y


def _ffn(x2, g, wg, wu, wd, fg, final_norm):
    n, d = x2.shape
    f = wg.shape[1]
    t = FFN_TOKENS
    return pl.pallas_call(
        functools.partial(_ffn_kernel, final_norm=final_norm),
        grid=(n // t,),
        in_specs=[
            pl.BlockSpec((t, d), lambda i: (i, 0)),
            _full((1, d)), _full((d, f)), _full((d, f)), _full((f, d)), _full((1, d)),
        ],
        out_specs=pl.BlockSpec((t, d), lambda i: (i, 0)),
        out_shape=jax.ShapeDtypeStruct((n, d), F32),
        compiler_params=_params(),
        name="ffn_final" if final_norm else "ffn",
    )(x2, g, wg, wu, wd, fg)


def _ab_kernel(x_ref, g_ref, win_ref, pw_ref, ps_ref, cw_ref, cb_ref, lg_ref, lb_ref, wout_ref,
               o_ref, ubuf, zbuf, zsh, ycat):
    t = x_ref.shape[1]
    pw_width = ps_ref.shape[1]
    cw_width = cb_ref.shape[1]
    j = pl.program_id(1)

    @pl.when(j == 0)
    def _():
        ubuf[0:HALO, :] = jnp.zeros((HALO, pw_width), F32)
        zbuf[0:HALO, :] = jnp.zeros((HALO, cw_width), F32)

    @pl.when(j > 0)
    def _():
        ubuf[0:HALO, :] = ubuf[t:t + HALO, :]
        zbuf[0:HALO, :] = zbuf[t:t + HALO, :]

    x = x_ref[0]
    h = _rms(x, g_ref[...]).astype(_MXU_DTYPE)
    proj = _mm(h, win_ref[...])
    u = proj[:, 0:pw_width]
    val = proj[:, pw_width:pw_width + cw_width]
    gate = proj[:, pw_width + cw_width:pw_width + 2 * cw_width]
    ubuf[HALO:HALO + t, :] = u
    zbuf[HALO:HALO + t, :] = val * jax.nn.sigmoid(gate)
    for r in range(1, SUBLANES):
        zsh[r - 1, SUBLANES:HALO + t, :] = zbuf[SUBLANES - r:HALO + t - r, :]

    pos = j * t + lax.broadcasted_iota(I32, (t, POOL_GROUP), 0)
    for g, w in enumerate(POOL_WINDOWS):
        lo, hi = g * POOL_GROUP, (g + 1) * POOL_GROUP
        ug = ubuf[HALO:HALO + t, lo:hi]
        acc = ug
        for back in range(1, w):
            acc = acc + ubuf[HALO - back:HALO - back + t, lo:hi]
        count = jnp.minimum(pos + 1, w).astype(F32)
        diff = (acc / count - ug).astype(_MXU_DTYPE)
        ycat[:, lo:hi] = (_mm(diff, pw_ref[g]) * ps_ref[:, lo:hi]).astype(ycat.dtype)

    def conv_block(rb, carry):
        base = rb * CONV_ROWS
        acc = jnp.broadcast_to(cb_ref[...], (CONV_ROWS, cw_width))
        for k in range(CONV_KERNEL):
            a, r = divmod(CONV_KERNEL - 1 - k, SUBLANES)
            start = pl.multiple_of(base + (HALO - a * SUBLANES), SUBLANES)
            src = zbuf if r == 0 else zsh.at[r - 1]
            acc = acc + src[pl.ds(start, CONV_ROWS), :] * cw_ref[k:k + 1, :]
        mu = jnp.mean(acc, axis=-1, keepdims=True)
        cen = acc - mu
        var = jnp.mean(cen * cen, axis=-1, keepdims=True)
        zn = cen * lax.rsqrt(var + NORM_EPS) * lg_ref[...] + lb_ref[...]
        ycat[pl.ds(pl.multiple_of(base, CONV_ROWS), CONV_ROWS), pw_width:pw_width + cw_width] = (
            zn * jax.nn.sigmoid(zn)).astype(ycat.dtype)
        return carry

    lax.fori_loop(0, t // CONV_ROWS, conv_block, 0)

    o_ref[0] = x + _mm(ycat[...], wout_ref[...])


def _ab_mixer(x, g, win, pw, ps, cw, cb, lg, lb, wout):
    b, s, d = x.shape
    t = MIX_TOKENS
    pwid, cwid = ps.shape[1], cb.shape[1]
    return pl.pallas_call(
        _ab_kernel,
        grid=(b, s // t),
        in_specs=[
            pl.BlockSpec((1, t, d), lambda bi, j: (bi, j, 0)),
            _full((1, d)), _full(win.shape), _full(pw.shape), _full(ps.shape), _full(cw.shape),
            _full(cb.shape), _full(lg.shape), _full(lb.shape), _full(wout.shape),
        ],
        out_specs=pl.BlockSpec((1, t, d), lambda bi, j: (bi, j, 0)),
        out_shape=jax.ShapeDtypeStruct((b, s, d), F32),
        scratch_shapes=[
            pltpu.VMEM((HALO + t, pwid), F32),
            pltpu.VMEM((HALO + t, cwid), F32),
            pltpu.VMEM((SUBLANES - 1, HALO + t, cwid), F32),
            pltpu.VMEM((t, pwid + cwid), _MXU_DTYPE),
        ],
        compiler_params=_params(),
        name="ab_mixer",
    )(x, g, win, pw, ps, cw, cb, lg, lb, wout)


def _cin_kernel(x_ref, g_ref, wn_ref, wt_ref, ww_ref, k_out, ki_out, qt_out, vt_out, qit_out, wt_out):
    t = x_ref.shape[1]
    aw = qt_out.shape[1]
    iw = qit_out.shape[1]
    h = _rms(x_ref[0], g_ref[...]).astype(_MXU_DTYPE)
    nat = _mm(h, wn_ref[...])
    for p in range(aw // LANES):
        k_out[0, p] = nat[:, p * LANES:(p + 1) * LANES].astype(k_out.dtype)
    ki_out[0] = nat[:, aw:aw + IDX_DIM].astype(ki_out.dtype)
    tr = _mm_nt(wt_ref[...], h)
    qt_out[0] = (tr[0:aw] * (HEAD_DIM ** -0.5 * LOG2E)).astype(qt_out.dtype)
    vt_out[0] = tr[aw:2 * aw].astype(vt_out.dtype)
    qit_out[0] = tr[2 * aw:2 * aw + iw].astype(qit_out.dtype)
    wt_out[0] = _mm_nt(ww_ref[...], h)


def _c_inproj(x, g, wn, wt, ww):
    b, s, d = x.shape
    t = MIX_TOKENS
    aw = N_HEADS * HEAD_DIM
    iw = N_IDX_HEADS * IDX_DIM
    md = _MXU_DTYPE
    return pl.pallas_call(
        _cin_kernel,
        grid=(b, s // t),
        in_specs=[
            pl.BlockSpec((1, t, d), lambda bi, j: (bi, j, 0)),
            _full((1, d)), _full(wn.shape), _full(wt.shape), _full(ww.shape),
        ],
        out_specs=[
            pl.BlockSpec((1, aw // LANES, t, LANES), lambda bi, j: (bi, 0, j, 0)),
            pl.BlockSpec((1, t, IDX_DIM), lambda bi, j: (bi, j, 0)),
            pl.BlockSpec((1, aw, t), lambda bi, j: (bi, 0, j)),
            pl.BlockSpec((1, aw, t), lambda bi, j: (bi, 0, j)),
            pl.BlockSpec((1, iw, t), lambda bi, j: (bi, 0, j)),
            pl.BlockSpec((1, N_IDX_HEADS, t), lambda bi, j: (bi, 0, j)),
        ],
        out_shape=[
            jax.ShapeDtypeStruct((b, aw // LANES, s, LANES), md),
            jax.ShapeDtypeStruct((b, s, IDX_DIM), md),
            jax.ShapeDtypeStruct((b, aw, s), md),
            jax.ShapeDtypeStruct((b, aw, s), md),
            jax.ShapeDtypeStruct((b, iw, s), md),
            jax.ShapeDtypeStruct((b, N_IDX_HEADS, s), F32),
        ],
        compiler_params=_params(),
        name="c_inproj",
    )(x, g, wn, wt, ww)


def _bias_kernel(rel_ref, bkt_ref, o_ref):
    h = pl.program_id(1)
    bkt = bkt_ref[0]
    acc = jnp.zeros(bkt.shape, F32)
    for b in range(REL_BUCKETS):
        acc = jnp.where(bkt == b, rel_ref[b, h], acc)
    o_ref[0, 0] = acc * LOG2E


def _bias_tiles(rel_bias, bkt):
    nd = bkt.shape[0]
    return pl.pallas_call(
        _bias_kernel,
        grid=(nd, N_HEADS),
        in_specs=[
            pl.BlockSpec(memory_space=pltpu.SMEM),
            pl.BlockSpec((1, KC, TQ), lambda dl, h: (dl, 0, 0)),
        ],
        out_specs=pl.BlockSpec((1, 1, KC, TQ), lambda dl, h: (dl, h, 0, 0)),
        out_shape=jax.ShapeDtypeStruct((nd, N_HEADS, KC, TQ), F32),
        name="bias_tiles",
    )(rel_bias, bkt)


def _t5_bucket(dist):
    max_exact = REL_BUCKETS // 2
    d = jnp.maximum(dist, max_exact).astype(F32)
    large = max_exact + (jnp.log(d / max_exact) / math.log(REL_MAX_DIST / max_exact)
                         * (REL_BUCKETS - max_exact)).astype(I32)
    large = jnp.minimum(large, REL_BUCKETS - 1)
    return jnp.where(dist < max_exact, dist, large)


def _colsum8(v):
    return v.reshape(v.shape[0] // SUBLANES, SUBLANES, v.shape[1]).sum(axis=0)


def _colmax8(v):
    return v.reshape(v.shape[0] // SUBLANES, SUBLANES, v.shape[1]).max(axis=0)


def _dsa_block(n, far_ref, qt_ref, qit_ref, wt_ref, k_ref, ki_ref, vt_ref, tile_ref,
               key_scr, madd_scr, logit_scr, ot_scr, *, top_k):
    idx_scale = (N_IDX_HEADS * IDX_DIM) ** -0.5
    causal = lax.broadcasted_iota(I32, (KC, TQ), 0) <= lax.broadcasted_iota(I32, (KC, TQ), 1)
    rows = lambda c: slice(c * KC, (c + 1) * KC)

    for c in range(n):
        ki_c = ki_ref[0, rows(c), :]
        acc = jnp.zeros((KC, TQ), F32)
        for h in range(N_IDX_HEADS):
            r = _mm(ki_c, qit_ref[0, h * IDX_DIM:(h + 1) * IDX_DIM, :])
            acc = acc + jnp.maximum(r, 0.0) * wt_ref[0, h:h + 1, :]
        bits = lax.bitcast_convert_type(acc * idx_scale, I32)
        key = jnp.where(bits < 0, bits ^ 0x7FFFFFFF, bits)
        if c == n - 1:
            key = jnp.where(causal, key, INT_MIN)
        key_scr[rows(c), :] = key

    if n * KC <= top_k:
        for c in range(n):
            madd_scr[rows(c), :] = jnp.where(key_scr[rows(c), :] != INT_MIN, 0.0, -jnp.inf)
    else:
        def count(pred, thr):
            acc = jnp.zeros((SUBLANES, TQ), I32)
            for c in range(n):
                acc = acc + _colsum8(pred(key_scr[rows(c), :], thr).astype(I32))
            return acc.sum(axis=0, keepdims=True)

        def bit_body(it, t_u):
            cand = t_u | lax.shift_left(jnp.int32(1), 31 - it)
            cnt = count(lambda k, t_: k >= t_, cand ^ INT_MIN)
            return jnp.where(cnt >= top_k, cand, t_u)

        thr = lax.fori_loop(0, 32, bit_body, jnp.zeros((1, TQ), I32)) ^ INT_MIN

        need = (top_k - count(lambda k, t_: k > t_, thr)).astype(F32)
        tri = (lax.broadcasted_iota(I32, (KC, KC), 1)
               < lax.broadcasted_iota(I32, (KC, KC), 0)).astype(_MXU_DTYPE)
        seen = jnp.zeros((1, TQ), F32)
        for c in range(n):
            k = key_scr[rows(c), :]
            eq = (k == thr).astype(F32)
            before = _mm(tri, eq.astype(_MXU_DTYPE)) + seen
            sel = ((k > thr) | ((eq > 0.0) & (before < need))) & (k != INT_MIN)
            madd_scr[rows(c), :] = jnp.where(sel, 0.0, -jnp.inf)
            seen = seen + _colsum8(eq).sum(axis=0, keepdims=True)

    pair_row = lax.broadcasted_iota(I32, (2 * HEAD_DIM, TQ), 0)
    ones_rows = jnp.ones((ONES_ROWS, KC), _MXU_DTYPE)
    n_far = max(n - 2, 0)

    def pair_body(p, carry):
        prow = pl.ds(pl.multiple_of(p * 2 * HEAD_DIM, 2 * HEAD_DIM), 2 * HEAD_DIM)
        qt = qt_ref[0, prow, :]
        zero = jnp.zeros_like(qt)
        q2 = jnp.concatenate([jnp.where(pair_row < HEAD_DIM, qt, zero),
                              jnp.where(pair_row >= HEAD_DIM, qt, zero)], axis=1)
        fars = [far_ref[2 * p + half] * LOG2E for half in range(2)]
        m_far = [jnp.full((SUBLANES, TQ), -jnp.inf, F32)] * 2
        m_near = list(m_far)
        for c in range(n):
            s2 = _mm(k_ref[0, p, rows(c), :], q2)
            for half in range(2):
                s = s2[:, half * TQ:(half + 1) * TQ] + madd_scr[rows(c), :]
                if c < n_far:
                    m_far[half] = jnp.maximum(m_far[half], _colmax8(s))
                else:
                    s = s + tile_ref[n - 1 - c, 2 * p + half]
                    m_near[half] = jnp.maximum(m_near[half], _colmax8(s))
                logit_scr[half, rows(c), :] = s
        m = [jnp.maximum(m_far[half] + fars[half], m_near[half]).max(axis=0, keepdims=True)
             for half in range(2)]
        acc = jnp.zeros((2 * HEAD_DIM + ONES_ROWS, 2 * TQ), F32)
        for c in range(n):
            e2 = jnp.concatenate(
                [jnp.exp2(logit_scr[half, rows(c), :]
                          - (m[half] - fars[half] if c < n_far else m[half])).astype(_MXU_DTYPE)
                 for half in range(2)], axis=1)
            lhs = jnp.concatenate([vt_ref[0, prow, rows(c)], ones_rows], axis=0)
            acc = acc + _mm(lhs, e2)
        for half in range(2):
            cols = slice(half * TQ, (half + 1) * TQ)
            ot_scr[pl.ds(pl.multiple_of((2 * p + half) * HEAD_DIM, HEAD_DIM), HEAD_DIM), :] = (
                acc[half * HEAD_DIM:(half + 1) * HEAD_DIM, cols]
                / acc[2 * HEAD_DIM:2 * HEAD_DIM + 1, cols])
        return carry

    lax.fori_loop(0, N_HEADS // 2, pair_body, 0)


def _dsa_kernel(far_ref, x_ref, qt_ref, qit_ref, wt_ref, k_ref, ki_ref, vt_ref, tile_ref, wout_ref,
                o_ref, key_scr, madd_scr, logit_scr, ot_scr, *, top_k):
    i = pl.program_id(1)
    for n in range(1, k_ref.shape[2] // KC + 1):
        pl.when(i == n - 1)(functools.partial(
            _dsa_block, n, far_ref, qt_ref, qit_ref, wt_ref, k_ref, ki_ref, vt_ref, tile_ref,
            key_scr, madd_scr, logit_scr, ot_scr, top_k=top_k))
    o_ref[0] = x_ref[0] + _mm_tn(ot_scr[...].astype(_MXU_DTYPE), wout_ref[...])


def _dsa(x, far, qt, qit, wt, k, ki, vt, tiles, wout, top_k):
    b, s, d = x.shape
    aw = N_HEADS * HEAD_DIM
    iw = N_IDX_HEADS * IDX_DIM
    return pl.pallas_call(
        functools.partial(_dsa_kernel, top_k=top_k),
        grid=(b, s // TQ),
        in_specs=[
            pl.BlockSpec(memory_space=pltpu.SMEM),
            pl.BlockSpec((1, TQ, d), lambda bi, i: (bi, i, 0)),
            pl.BlockSpec((1, aw, TQ), lambda bi, i: (bi, 0, i)),
            pl.BlockSpec((1, iw, TQ), lambda bi, i: (bi, 0, i)),
            pl.BlockSpec((1, N_IDX_HEADS, TQ), lambda bi, i: (bi, 0, i)),
            pl.BlockSpec((1, aw // LANES, s, LANES), lambda bi, i: (bi, 0, 0, 0)),
            pl.BlockSpec((1, s, IDX_DIM), lambda bi, i: (bi, 0, 0)),
            pl.BlockSpec((1, aw, s), lambda bi, i: (bi, 0, 0)),
            _full(tiles.shape), _full(wout.shape),
        ],
        out_specs=pl.BlockSpec((1, TQ, d), lambda bi, i: (bi, i, 0)),
        out_shape=jax.ShapeDtypeStruct((b, s, d), F32),
        scratch_shapes=[
            pltpu.VMEM((s, TQ), I32),
            pltpu.VMEM((s, TQ), F32),
            pltpu.VMEM((2, s, TQ), F32),
            pltpu.VMEM((aw, TQ), F32),
        ],
        compiler_params=_params(),
        name="dsa",
    )(far, x, qt, qit, wt, k, ki, vt, tiles, wout)


def kernel(x, ffn1_norm, ffn1_w_gate, ffn1_w_up, ffn1_w_down, mix_norm, ffn2_norm, ffn2_w_gate,
           ffn2_w_up, ffn2_w_down, ab_w_in, pool_w, pool_scale, conv_w, conv_b, conv_ln_g,
           conv_ln_b, ab_w_out, c_w_in, c_w_out, rel_bias, final_norm):
    b, s, d = x.shape
    depth = ffn1_norm.shape[0]
    md = _MXU_DTYPE
    aw = N_HEADS * HEAD_DIM
    iw = N_IDX_HEADS * IDX_DIM
    top_k = min(TOPK_MAX, s // 4)
    assert s % MIX_TOKENS == 0 and (b * s) % FFN_TOKENS == 0 and s % TQ == 0 and top_k <= KC

    row = lambda v: v.reshape(1, -1)

    if depth > 1:
        sk = jnp.arange(KC, dtype=I32)[:, None]
        tq = jnp.arange(TQ, dtype=I32)[None, :]
        bkt = jnp.stack([_t5_bucket(jnp.maximum(tq + dl * TQ - sk, 0)) for dl in (0, 1)])
        tiles = _bias_tiles(rel_bias, bkt)
        assert 2 * TQ - KC + 1 >= REL_MAX_DIST
        far = rel_bias[REL_BUCKETS - 1]

    for layer in range(depth):
        i = layer // 2
        last = layer == depth - 1
        x2 = _ffn(x.reshape(b * s, d), row(ffn1_norm[layer]), ffn1_w_gate[layer].astype(md),
                  ffn1_w_up[layer].astype(md), ffn1_w_down[layer].astype(md), row(final_norm), False)
        x = x2.reshape(b, s, d)
        if layer % 2 == 0:
            x = _ab_mixer(x, row(mix_norm[layer]), ab_w_in[i].astype(md), pool_w[i].astype(md),
                          row(pool_scale[i]), conv_w[i], row(conv_b[i]), row(conv_ln_g[i]),
                          row(conv_ln_b[i]), ab_w_out[i].astype(md))
        else:
            w = c_w_in[i]
            wq, wk, wv = w[:, 0:aw], w[:, aw:2 * aw], w[:, 2 * aw:3 * aw]
            wqi = w[:, 3 * aw:3 * aw + iw]
            wki = w[:, 3 * aw + iw:3 * aw + iw + IDX_DIM]
            www = w[:, 3 * aw + iw + IDX_DIM:]
            wn = jnp.concatenate([wk, wki], axis=1).astype(md)
            wt = jnp.concatenate([wq, wv, wqi], axis=1).T.astype(md)
            k, ki, qt, vt, qit, wts = _c_inproj(x, row(mix_norm[layer]), wn, wt, www.T.astype(md))
            x = _dsa(x, far, qt, qit, wts, k, ki, vt, tiles, c_w_out[i].astype(md), top_k)
        x2 = _ffn(x.reshape(b * s, d), row(ffn2_norm[layer]), ffn2_w_gate[layer].astype(md),
                  ffn2_w_up[layer].astype(md), ffn2_w_down[layer].astype(md), row(final_norm), last)
        x = x2.reshape(b, s, d)
    return x
```

```python
import functools
import math

import jax
import jax.numpy as jnp
from jax import lax
from jax.experimental import pallas as pl
from jax.experimental.pallas import tpu as pltpu

F32 = jnp.float32
I32 = jnp.int32
I16 = jnp.int16
_MXU_DTYPE = jnp.bfloat16

NORM_EPS = 1e-6
FFN_RES = 0.5
POOL_WINDOWS = (2, 4, 8, 16)
POOL_GROUP = 128
CONV_KERNEL = 31
N_HEADS = 16
HEAD_DIM = 64
N_IDX_HEADS = 8
IDX_DIM = 64
TOPK_MAX = 256
REL_BUCKETS = 32
REL_MAX_DIST = 128

V7X_VMEM_BYTES = 64 * 1024 * 1024
VMEM_LIMIT = V7X_VMEM_BYTES - 6 * 1024 * 1024
SUBLANES = 8
LANES = 128

FFN_TOKENS = 512
MIX_TOKENS = 512
CONV_ROWS = 32
HALO = 32
TQ = 256
KC = 256
INT_MIN = -2 ** 31
HALF16 = 2 ** 15
PACK16 = 16
LOG2E = 1.0 / math.log(2.0)
PAIRS_PER_STEP = 2
ONES_ROWS = 16


def _rms(x, g):
    return x * lax.rsqrt(jnp.mean(x * x, axis=-1, keepdims=True) + NORM_EPS) * g


def _mm(a, b):
    return jnp.dot(a, b, preferred_element_type=F32)


def _mm_nt(a, b):
    return lax.dot_general(a, b, (((1,), (1,)), ((), ())), preferred_element_type=F32)


def _mm_tn(a, b):
    return lax.dot_general(a, b, (((0,), (0,)), ((), ())), preferred_element_type=F32)


def _params():
    return pltpu.CompilerParams(vmem_limit_bytes=VMEM_LIMIT)


def _full(shape):
    n = len(shape)
    return pl.BlockSpec(shape, lambda *_: (0,) * n)


def _ffn_kernel(x_ref, g_ref, wg_ref, wu_ref, wd_ref, fg_ref, o_ref, *, final_norm):
    x = x_ref[...]
    h = _rms(x, g_ref[...]).astype(_MXU_DTYPE)
    gate = _mm(h, wg_ref[...])
    up = _mm(h, wu_ref[...])
    act = (gate * jax.nn.sigmoid(gate) * up).astype(_MXU_DTYPE)
    y = x + FFN_RES * _mm(act, wd_ref[...])
    if final_norm:
        y = _rms(y, fg_ref[...])
    o_ref[...] = y


def _ffn(x2, g, wg, wu, wd, fg, final_norm):
    n, d = x2.shape
    f = wg.shape[1]
    t = FFN_TOKENS
    return pl.pallas_call(
        functools.partial(_ffn_kernel, final_norm=final_norm),
        grid=(n // t,),
        in_specs=[
            pl.BlockSpec((t, d), lambda i: (i, 0)),
            _full((1, d)), _full((d, f)), _full((d, f)), _full((f, d)), _full((1, d)),
        ],
        out_specs=pl.BlockSpec((t, d), lambda i: (i, 0)),
        out_shape=jax.ShapeDtypeStruct((n, d), F32),
        compiler_params=_params(),
        name="ffn_final" if final_norm else "ffn",
    )(x2, g, wg, wu, wd, fg)


def _ab_kernel(x_ref, g_ref, win_ref, pw_ref, ps_ref, cw_ref, cb_ref, lg_ref, lb_ref, wout_ref,
               o_ref, ubuf, zbuf, zsh, ycat):
    t = x_ref.shape[1]
    pw_width = ps_ref.shape[1]
    cw_width = cb_ref.shape[1]
    j = pl.program_id(1)

    @pl.when(j == 0)
    def _():
        ubuf[0:HALO, :] = jnp.zeros((HALO, pw_width), F32)
        zbuf[0:HALO, :] = jnp.zeros((HALO, cw_width), F32)

    @pl.when(j > 0)
    def _():
        ubuf[0:HALO, :] = ubuf[t:t + HALO, :]
        zbuf[0:HALO, :] = zbuf[t:t + HALO, :]

    x = x_ref[0]
    h = _rms(x, g_ref[...]).astype(_MXU_DTYPE)
    proj = _mm(h, win_ref[...])
    u = proj[:, 0:pw_width]
    val = proj[:, pw_width:pw_width + cw_width]
    gate = proj[:, pw_width + cw_width:pw_width + 2 * cw_width]
    ubuf[HALO:HALO + t, :] = u
    zbuf[HALO:HALO + t, :] = val * jax.nn.sigmoid(gate)
    for r in range(1, SUBLANES):
        zsh[r - 1, SUBLANES:HALO + t, :] = zbuf[SUBLANES - r:HALO + t - r, :]

    pos = j * t + lax.broadcasted_iota(I32, (t, POOL_GROUP), 0)
    for g, w in enumerate(POOL_WINDOWS):
        lo, hi = g * POOL_GROUP, (g + 1) * POOL_GROUP
        ug = ubuf[HALO:HALO + t, lo:hi]
        acc = ug
        for back in range(1, w):
            acc = acc + ubuf[HALO - back:HALO - back + t, lo:hi]
        count = jnp.minimum(pos + 1, w).astype(F32)
        diff = (acc / count - ug).astype(_MXU_DTYPE)
        ycat[:, lo:hi] = (_mm(diff, pw_ref[g]) * ps_ref[:, lo:hi]).astype(ycat.dtype)

    def conv_block(rb, carry):
        base = rb * CONV_ROWS
        groups = CONV_ROWS // SUBLANES
        acc = jnp.broadcast_to(cb_ref[...].reshape(1, 1, cw_width), (groups, SUBLANES, cw_width))
        for k in range(CONV_KERNEL):
            a, r = divmod(CONV_KERNEL - 1 - k, SUBLANES)
            start = pl.multiple_of(base + (HALO - a * SUBLANES), SUBLANES)
            src = zbuf if r == 0 else zsh.at[r - 1]
            win = src[pl.ds(start, CONV_ROWS), :].reshape(groups, SUBLANES, cw_width)
            acc = acc + win * cw_ref[k][None]
        acc = acc.reshape(CONV_ROWS, cw_width)
        mu = jnp.mean(acc, axis=-1, keepdims=True)
        cen = acc - mu
        var = jnp.mean(cen * cen, axis=-1, keepdims=True)
        zn = cen * lax.rsqrt(var + NORM_EPS) * lg_ref[...] + lb_ref[...]
        ycat[pl.ds(pl.multiple_of(base, CONV_ROWS), CONV_ROWS), pw_width:pw_width + cw_width] = (
            zn * jax.nn.sigmoid(zn)).astype(ycat.dtype)
        return carry

    lax.fori_loop(0, t // CONV_ROWS, conv_block, 0, unroll=2)

    o_ref[0] = x + _mm(ycat[...], wout_ref[...])


def _ab_mixer(x, g, win, pw, ps, cw, cb, lg, lb, wout):
    b, s, d = x.shape
    t = MIX_TOKENS
    pwid, cwid = ps.shape[1], cb.shape[1]
    return pl.pallas_call(
        _ab_kernel,
        grid=(b, s // t),
        in_specs=[
            pl.BlockSpec((1, t, d), lambda bi, j: (bi, j, 0)),
            _full((1, d)), _full(win.shape), _full(pw.shape), _full(ps.shape), _full(cw.shape),
            _full(cb.shape), _full(lg.shape), _full(lb.shape), _full(wout.shape),
        ],
        out_specs=pl.BlockSpec((1, t, d), lambda bi, j: (bi, j, 0)),
        out_shape=jax.ShapeDtypeStruct((b, s, d), F32),
        scratch_shapes=[
            pltpu.VMEM((HALO + t, pwid), F32),
            pltpu.VMEM((HALO + t, cwid), F32),
            pltpu.VMEM((SUBLANES - 1, HALO + t, cwid), F32),
            pltpu.VMEM((t, pwid + cwid), _MXU_DTYPE),
        ],
        compiler_params=_params(),
        name="ab_mixer",
    )(x, g, win, pw, ps, cw, cb, lg, lb, wout)


def _cin_kernel(x_ref, g_ref, wn_ref, wt_ref, ww_ref, k_out, ki_out, qt_out, vt_out, qit_out, wt_out):
    t = x_ref.shape[1]
    aw = qt_out.shape[1]
    iw = qit_out.shape[1]
    h = _rms(x_ref[0], g_ref[...]).astype(_MXU_DTYPE)
    nat = _mm(h, wn_ref[...])
    for p in range(aw // LANES):
        k_out[0, p] = nat[:, p * LANES:(p + 1) * LANES].astype(k_out.dtype)
    ki_out[0] = nat[:, aw:aw + IDX_DIM].astype(ki_out.dtype)
    tr = _mm_nt(wt_ref[...], h)
    qt_out[0] = (tr[0:aw] * (HEAD_DIM ** -0.5 * LOG2E)).astype(qt_out.dtype)
    vt_out[0] = tr[aw:2 * aw].astype(vt_out.dtype)
    qit_out[0] = tr[2 * aw:2 * aw + iw].astype(qit_out.dtype)
    wt_out[0] = _mm_nt(ww_ref[...], h)


def _c_inproj(x, g, wn, wt, ww):
    b, s, d = x.shape
    t = MIX_TOKENS
    aw = N_HEADS * HEAD_DIM
    iw = N_IDX_HEADS * IDX_DIM
    md = _MXU_DTYPE
    return pl.pallas_call(
        _cin_kernel,
        grid=(b, s // t),
        in_specs=[
            pl.BlockSpec((1, t, d), lambda bi, j: (bi, j, 0)),
            _full((1, d)), _full(wn.shape), _full(wt.shape), _full(ww.shape),
        ],
        out_specs=[
            pl.BlockSpec((1, aw // LANES, t, LANES), lambda bi, j: (bi, 0, j, 0)),
            pl.BlockSpec((1, t, IDX_DIM), lambda bi, j: (bi, j, 0)),
            pl.BlockSpec((1, aw, t), lambda bi, j: (bi, 0, j)),
            pl.BlockSpec((1, aw, t), lambda bi, j: (bi, 0, j)),
            pl.BlockSpec((1, iw, t), lambda bi, j: (bi, 0, j)),
            pl.BlockSpec((1, N_IDX_HEADS, t), lambda bi, j: (bi, 0, j)),
        ],
        out_shape=[
            jax.ShapeDtypeStruct((b, aw // LANES, s, LANES), md),
            jax.ShapeDtypeStruct((b, s, IDX_DIM), md),
            jax.ShapeDtypeStruct((b, aw, s), md),
            jax.ShapeDtypeStruct((b, aw, s), md),
            jax.ShapeDtypeStruct((b, iw, s), md),
            jax.ShapeDtypeStruct((b, N_IDX_HEADS, s), F32),
        ],
        compiler_params=_params(),
        name="c_inproj",
    )(x, g, wn, wt, ww)


def _bias_kernel(rel_ref, bkt_ref, o_ref):
    h = pl.program_id(1)
    bkt = bkt_ref[0]
    acc = jnp.zeros(bkt.shape, F32)
    for b in range(REL_BUCKETS):
        acc = jnp.where(bkt == b, rel_ref[b, h], acc)
    o_ref[0, 0] = acc * LOG2E


def _bias_tiles(rel_bias, bkt):
    nd = bkt.shape[0]
    return pl.pallas_call(
        _bias_kernel,
        grid=(nd, N_HEADS),
        in_specs=[
            pl.BlockSpec(memory_space=pltpu.SMEM),
            pl.BlockSpec((1, KC, TQ), lambda dl, h: (dl, 0, 0)),
        ],
        out_specs=pl.BlockSpec((1, 1, KC, TQ), lambda dl, h: (dl, h, 0, 0)),
        out_shape=jax.ShapeDtypeStruct((nd, N_HEADS, KC, TQ), F32),
        name="bias_tiles",
    )(rel_bias, bkt)


def _t5_bucket(dist):
    max_exact = REL_BUCKETS // 2
    d = jnp.maximum(dist, max_exact).astype(F32)
    large = max_exact + (jnp.log(d / max_exact) / math.log(REL_MAX_DIST / max_exact)
                         * (REL_BUCKETS - max_exact)).astype(I32)
    large = jnp.minimum(large, REL_BUCKETS - 1)
    return jnp.where(dist < max_exact, dist, large)


def _colsum8(v):
    return v.reshape(v.shape[0] // SUBLANES, SUBLANES, v.shape[1]).sum(axis=0)


def _colmax8(v):
    return v.reshape(v.shape[0] // SUBLANES, SUBLANES, v.shape[1]).max(axis=0)


def _dsa_block(n, far_ref, qt_ref, qit_ref, wt_ref, k_ref, ki_ref, vt_ref, tile_ref,
               key_scr, hi_scr, lo_scr, madd_scr, logit_scr, ot_scr, *, top_k):
    idx_scale = (N_IDX_HEADS * IDX_DIM) ** -0.5
    causal = lax.broadcasted_iota(I32, (KC, TQ), 0) <= lax.broadcasted_iota(I32, (KC, TQ), 1)
    rows = lambda c: slice(c * KC, (c + 1) * KC)

    for c in range(n):
        ki_c = ki_ref[0, rows(c), :]
        acc = jnp.zeros((KC, TQ), F32)
        for h in range(N_IDX_HEADS):
            r = _mm(ki_c, qit_ref[0, h * IDX_DIM:(h + 1) * IDX_DIM, :])
            acc = acc + jnp.maximum(r, 0.0) * wt_ref[0, h:h + 1, :]
        bits = lax.bitcast_convert_type(acc * idx_scale, I32)
        key = jnp.where(bits < 0, bits ^ 0x7FFFFFFF, bits)
        if c == n - 1:
            key = jnp.where(causal, key, INT_MIN)
        key_scr[rows(c), :] = key
        hi_scr[rows(c), :] = (key >> 16).astype(I16)

    if n * KC <= top_k:
        for c in range(n):
            madd_scr[rows(c), :] = jnp.where(key_scr[rows(c), :] != INT_MIN, 0.0, -jnp.inf)
    else:
        def count16(ref, pred, t16):
            acc = jnp.zeros((PACK16, TQ), I16)
            for c in range(n):
                hit = pred(ref[rows(c), :], t16).astype(I16)
                for r in range(KC // PACK16):
                    acc = acc + hit[r * PACK16:(r + 1) * PACK16]
            return acc.astype(I32).sum(axis=0, keepdims=True)

        def search16(ref, want):
            def bit_body(it, t_u):
                cand = t_u | lax.shift_left(jnp.int32(1), 15 - it)
                cnt = count16(ref, lambda v, t_: v >= t_, (cand - HALF16).astype(I16))
                return jnp.where(cnt >= want, cand, t_u)
            return lax.fori_loop(0, 16, bit_body, jnp.zeros((1, TQ), I32)) - HALF16

        t_hi = search16(hi_scr, top_k)
        above = count16(hi_scr, lambda v, t_: v > t_, t_hi.astype(I16))
        for c in range(n):
            k = key_scr[rows(c), :]
            lo = jnp.where((k >> 16) == t_hi, (k & 0xFFFF) - HALF16, -HALF16)
            lo_scr[rows(c), :] = lo.astype(I16)
        t_lo = search16(lo_scr, top_k - above)
        thr = lax.shift_left(t_hi, 16) | (t_lo + HALF16)

        def count(pred, t_):
            acc = jnp.zeros((SUBLANES, TQ), I32)
            for c in range(n):
                acc = acc + _colsum8(pred(key_scr[rows(c), :], t_).astype(I32))
            return acc.sum(axis=0, keepdims=True)

        need = (top_k - count(lambda k, t_: k > t_, thr)).astype(F32)
        tri = (lax.broadcasted_iota(I32, (KC, KC), 1)
               < lax.broadcasted_iota(I32, (KC, KC), 0)).astype(_MXU_DTYPE)
        seen = jnp.zeros((1, TQ), F32)
        for c in range(n):
            k = key_scr[rows(c), :]
            eq = (k == thr).astype(F32)
            before = _mm(tri, eq.astype(_MXU_DTYPE)) + seen
            sel = ((k > thr) | ((eq > 0.0) & (before < need))) & (k != INT_MIN)
            madd_scr[rows(c), :] = jnp.where(sel, 0.0, -jnp.inf)
            seen = seen + _colsum8(eq).sum(axis=0, keepdims=True)

    pair_row = lax.broadcasted_iota(I32, (2 * HEAD_DIM, TQ), 0)
    ones_rows = jnp.ones((ONES_ROWS, KC), _MXU_DTYPE)
    n_far = max(n - 2, 0)

    def scores(p, slot):
        prow = pl.ds(pl.multiple_of(p * 2 * HEAD_DIM, 2 * HEAD_DIM), 2 * HEAD_DIM)
        qt = qt_ref[0, prow, :]
        zero = jnp.zeros_like(qt)
        q2 = jnp.concatenate([jnp.where(pair_row < HEAD_DIM, qt, zero),
                              jnp.where(pair_row >= HEAD_DIM, qt, zero)], axis=1)
        fars = [far_ref[2 * p + half] * LOG2E for half in range(2)]
        m_far = [jnp.full((SUBLANES, TQ), -jnp.inf, F32)] * 2
        m_near = list(m_far)
        for c in range(n):
            s2 = _mm(k_ref[0, p, rows(c), :], q2)
            for half in range(2):
                s = s2[:, half * TQ:(half + 1) * TQ] + madd_scr[rows(c), :]
                if c < n_far:
                    m_far[half] = jnp.maximum(m_far[half], _colmax8(s))
                else:
                    s = s + tile_ref[n - 1 - c, 2 * p + half]
                    m_near[half] = jnp.maximum(m_near[half], _colmax8(s))
                logit_scr[2 * slot + half, rows(c), :] = s
        return [jnp.maximum(m_far[half] + fars[half], m_near[half]).max(axis=0, keepdims=True)
                for half in range(2)]

    def weighted_values(p, slot, m):
        prow = pl.ds(pl.multiple_of(p * 2 * HEAD_DIM, 2 * HEAD_DIM), 2 * HEAD_DIM)
        fars = [far_ref[2 * p + half] * LOG2E for half in range(2)]
        acc = jnp.zeros((2 * HEAD_DIM + ONES_ROWS, 2 * TQ), F32)
        for c in range(n):
            e2 = jnp.concatenate(
                [jnp.exp2(logit_scr[2 * slot + half, rows(c), :]
                          - (m[half] - fars[half] if c < n_far else m[half])).astype(_MXU_DTYPE)
                 for half in range(2)], axis=1)
            lhs = jnp.concatenate([vt_ref[0, prow, rows(c)], ones_rows], axis=0)
            acc = acc + _mm(lhs, e2)
        for half in range(2):
            cols = slice(half * TQ, (half + 1) * TQ)
            ot_scr[pl.ds(pl.multiple_of((2 * p + half) * HEAD_DIM, HEAD_DIM), HEAD_DIM), :] = (
                acc[half * HEAD_DIM:(half + 1) * HEAD_DIM, cols]
                / acc[2 * HEAD_DIM:2 * HEAD_DIM + 1, cols])

    def group_body(g, carry):
        ms = [scores(g * PAIRS_PER_STEP + u, u) for u in range(PAIRS_PER_STEP)]
        for u in range(PAIRS_PER_STEP):
            weighted_values(g * PAIRS_PER_STEP + u, u, ms[u])
        return carry

    lax.fori_loop(0, N_HEADS // 2 // PAIRS_PER_STEP, group_body, 0)


def _dsa_kernel(far_ref, x_ref, qt_ref, qit_ref, wt_ref, k_ref, ki_ref, vt_ref, tile_ref, wout_ref,
                o_ref, key_scr, hi_scr, lo_scr, madd_scr, logit_scr, ot_scr, *, top_k):
    i = pl.program_id(1)
    for n in range(1, k_ref.shape[2] // KC + 1):
        pl.when(i == n - 1)(functools.partial(
            _dsa_block, n, far_ref, qt_ref, qit_ref, wt_ref, k_ref, ki_ref, vt_ref, tile_ref,
            key_scr, hi_scr, lo_scr, madd_scr, logit_scr, ot_scr, top_k=top_k))
    o_ref[0] = x_ref[0] + _mm_tn(ot_scr[...].astype(_MXU_DTYPE), wout_ref[...])


def _dsa(x, far, qt, qit, wt, k, ki, vt, tiles, wout, top_k):
    b, s, d = x.shape
    aw = N_HEADS * HEAD_DIM
    iw = N_IDX_HEADS * IDX_DIM
    return pl.pallas_call(
        functools.partial(_dsa_kernel, top_k=top_k),
        grid=(b, s // TQ),
        in_specs=[
            pl.BlockSpec(memory_space=pltpu.SMEM),
            pl.BlockSpec((1, TQ, d), lambda bi, i: (bi, i, 0)),
            pl.BlockSpec((1, aw, TQ), lambda bi, i: (bi, 0, i)),
            pl.BlockSpec((1, iw, TQ), lambda bi, i: (bi, 0, i)),
            pl.BlockSpec((1, N_IDX_HEADS, TQ), lambda bi, i: (bi, 0, i)),
            pl.BlockSpec((1, aw // LANES, s, LANES), lambda bi, i: (bi, 0, 0, 0)),
            pl.BlockSpec((1, s, IDX_DIM), lambda bi, i: (bi, 0, 0)),
            pl.BlockSpec((1, aw, s), lambda bi, i: (bi, 0, 0)),
            _full(tiles.shape), _full(wout.shape),
        ],
        out_specs=pl.BlockSpec((1, TQ, d), lambda bi, i: (bi, i, 0)),
        out_shape=jax.ShapeDtypeStruct((b, s, d), F32),
        scratch_shapes=[
            pltpu.VMEM((s, TQ), I32),
            pltpu.VMEM((s, TQ), I16),
            pltpu.VMEM((s, TQ), I16),
            pltpu.VMEM((s, TQ), F32),
            pltpu.VMEM((2 * PAIRS_PER_STEP, s, TQ), F32),
            pltpu.VMEM((aw, TQ), F32),
        ],
        compiler_params=_params(),
        name="dsa",
    )(far, x, qt, qit, wt, k, ki, vt, tiles, wout)


def kernel(x, ffn1_norm, ffn1_w_gate, ffn1_w_up, ffn1_w_down, mix_norm, ffn2_norm, ffn2_w_gate,
           ffn2_w_up, ffn2_w_down, ab_w_in, pool_w, pool_scale, conv_w, conv_b, conv_ln_g,
           conv_ln_b, ab_w_out, c_w_in, c_w_out, rel_bias, final_norm):
    b, s, d = x.shape
    depth = ffn1_norm.shape[0]
    md = _MXU_DTYPE
    aw = N_HEADS * HEAD_DIM
    iw = N_IDX_HEADS * IDX_DIM
    top_k = min(TOPK_MAX, s // 4)
    assert s % MIX_TOKENS == 0 and (b * s) % FFN_TOKENS == 0 and s % TQ == 0 and top_k <= KC

    row = lambda v: v.reshape(1, -1)

    if depth > 1:
        sk = jnp.arange(KC, dtype=I32)[:, None]
        tq = jnp.arange(TQ, dtype=I32)[None, :]
        bkt = jnp.stack([_t5_bucket(jnp.maximum(tq + dl * TQ - sk, 0)) for dl in (0, 1)])
        tiles = _bias_tiles(rel_bias, bkt)
        assert 2 * TQ - KC + 1 >= REL_MAX_DIST
        far = rel_bias[REL_BUCKETS - 1]

    for layer in range(depth):
        i = layer // 2
        last = layer == depth - 1
        x2 = _ffn(x.reshape(b * s, d), row(ffn1_norm[layer]), ffn1_w_gate[layer].astype(md),
                  ffn1_w_up[layer].astype(md), ffn1_w_down[layer].astype(md), row(final_norm), False)
        x = x2.reshape(b, s, d)
        if layer % 2 == 0:
            x = _ab_mixer(x, row(mix_norm[layer]), ab_w_in[i].astype(md), pool_w[i].astype(md),
                          row(pool_scale[i]),
                          jnp.broadcast_to(conv_w[i][:, None, :], (CONV_KERNEL, SUBLANES, conv_w.shape[-1])),
                          row(conv_b[i]), row(conv_ln_g[i]),
                          row(conv_ln_b[i]), ab_w_out[i].astype(md))
        else:
            w = c_w_in[i]
            wq, wk, wv = w[:, 0:aw], w[:, aw:2 * aw], w[:, 2 * aw:3 * aw]
            wqi = w[:, 3 * aw:3 * aw + iw]
            wki = w[:, 3 * aw + iw:3 * aw + iw + IDX_DIM]
            www = w[:, 3 * aw + iw + IDX_DIM:]
            wn = jnp.concatenate([wk, wki], axis=1).astype(md)
            wt = jnp.concatenate([wq, wv, wqi], axis=1).T.astype(md)
            k, ki, qt, vt, qit, wts = _c_inproj(x, row(mix_norm[layer]), wn, wt, www.T.astype(md))
            x = _dsa(x, far, qt, qit, wts, k, ki, vt, tiles, c_w_out[i].astype(md), top_k)
        x2 = _ffn(x.reshape(b * s, d), row(ffn2_norm[layer]), ffn2_w_gate[layer].astype(md),
                  ffn2_w_up[layer].astype(md), ffn2_w_down[layer].astype(md), row(final_norm), last)
        x = x2.reshape(b, s, d)
    return x
```

```python
import functools
import math

import jax
import jax.numpy as jnp
from jax import lax
from jax.experimental import pallas as pl
from jax.experimental.pallas import tpu as pltpu

F32 = jnp.float32
I32 = jnp.int32
I16 = jnp.int16
_MXU_DTYPE = jnp.bfloat16

NORM_EPS = 1e-6
FFN_RES = 0.5
POOL_WINDOWS = (2, 4, 8, 16)
POOL_GROUP = 128
CONV_KERNEL = 31
N_HEADS = 16
HEAD_DIM = 64
N_IDX_HEADS = 8
IDX_DIM = 64
TOPK_MAX = 256
REL_BUCKETS = 32
REL_MAX_DIST = 128

V7X_VMEM_BYTES = 64 * 1024 * 1024
VMEM_LIMIT = V7X_VMEM_BYTES - 6 * 1024 * 1024
SUBLANES = 8
LANES = 128

FFN_TOKENS = 512
MIX_TOKENS = 512
CONV_ROWS = 32
HALO = 32
TQ = 256
KC = 256
INT_MIN = -2 ** 31
HALF16 = 2 ** 15
PACK16 = 16
LOG2E = 1.0 / math.log(2.0)
PAIRS_PER_STEP = 2
ONES_ROWS = 16


def _rms(x, g):
    return x * lax.rsqrt(jnp.mean(x * x, axis=-1, keepdims=True) + NORM_EPS) * g


def _mm(a, b):
    return jnp.dot(a, b, preferred_element_type=F32)


def _mm_nt(a, b):
    return lax.dot_general(a, b, (((1,), (1,)), ((), ())), preferred_element_type=F32)


def _mm_tn(a, b):
    return lax.dot_general(a, b, (((0,), (0,)), ((), ())), preferred_element_type=F32)


def _params():
    return pltpu.CompilerParams(vmem_limit_bytes=VMEM_LIMIT)


def _full(shape):
    n = len(shape)
    return pl.BlockSpec(shape, lambda *_: (0,) * n)


def _ffn_kernel(x_ref, g_ref, wg_ref, wu_ref, wd_ref, fg_ref, o_ref, *, final_norm):
    x = x_ref[...]
    h = _rms(x, g_ref[...]).astype(_MXU_DTYPE)
    gate = _mm(h, wg_ref[...])
    up = _mm(h, wu_ref[...])
    act = (gate * jax.nn.sigmoid(gate) * up).astype(_MXU_DTYPE)
    y = x + FFN_RES * _mm(act, wd_ref[...])
    if final_norm:
        y = _rms(y, fg_ref[...])
    o_ref[...] = y


def _ffn(x2, g, wg, wu, wd, fg, final_norm):
    n, d = x2.shape
    f = wg.shape[1]
    t = FFN_TOKENS
    return pl.pallas_call(
        functools.partial(_ffn_kernel, final_norm=final_norm),
        grid=(n // t,),
        in_specs=[
            pl.BlockSpec((t, d), lambda i: (i, 0)),
            _full((1, d)), _full((d, f)), _full((d, f)), _full((f, d)), _full((1, d)),
        ],
        out_specs=pl.BlockSpec((t, d), lambda i: (i, 0)),
        out_shape=jax.ShapeDtypeStruct((n, d), F32),
        compiler_params=_params(),
        name="ffn_final" if final_norm else "ffn",
    )(x2, g, wg, wu, wd, fg)


def _ab_kernel(x_ref, g_ref, win_ref, pw_ref, ps_ref, cw_ref, cb_ref, lg_ref, lb_ref, wout_ref,
               o_ref, ubuf, zbuf, zsh, ycat, conv_scr):
    t = x_ref.shape[1]
    pw_width = ps_ref.shape[1]
    cw_width = cb_ref.shape[1]
    j = pl.program_id(1)

    @pl.when(j == 0)
    def _():
        ubuf[0:HALO, :] = jnp.zeros((HALO, pw_width), F32)
        zbuf[0:HALO, :] = jnp.zeros((HALO, cw_width), F32)

    @pl.when(j > 0)
    def _():
        ubuf[0:HALO, :] = ubuf[t:t + HALO, :]
        zbuf[0:HALO, :] = zbuf[t:t + HALO, :]

    x = x_ref[0]
    h = _rms(x, g_ref[...]).astype(_MXU_DTYPE)
    proj = _mm(h, win_ref[...])
    u = proj[:, 0:pw_width]
    val = proj[:, pw_width:pw_width + cw_width]
    gate = proj[:, pw_width + cw_width:pw_width + 2 * cw_width]
    ubuf[HALO:HALO + t, :] = u
    zbuf[HALO:HALO + t, :] = val * jax.nn.sigmoid(gate)
    for r in range(1, SUBLANES):
        zsh[r - 1, SUBLANES:HALO + t, :] = zbuf[SUBLANES - r:HALO + t - r, :]

    pos = j * t + lax.broadcasted_iota(I32, (t, POOL_GROUP), 0)
    for g, w in enumerate(POOL_WINDOWS):
        lo, hi = g * POOL_GROUP, (g + 1) * POOL_GROUP
        ug = ubuf[HALO:HALO + t, lo:hi]
        acc = ug
        for back in range(1, w):
            acc = acc + ubuf[HALO - back:HALO - back + t, lo:hi]
        count = jnp.minimum(pos + 1, w).astype(F32)
        diff = (acc / count - ug).astype(_MXU_DTYPE)
        ycat[:, lo:hi] = (_mm(diff, pw_ref[g]) * ps_ref[:, lo:hi]).astype(ycat.dtype)

    def conv_block(rb, carry):
        base = rb * CONV_ROWS
        groups = CONV_ROWS // SUBLANES
        acc = jnp.broadcast_to(cb_ref[...].reshape(1, 1, cw_width), (groups, SUBLANES, cw_width))
        for k in range(CONV_KERNEL):
            a, r = divmod(CONV_KERNEL - 1 - k, SUBLANES)
            start = pl.multiple_of(base + (HALO - a * SUBLANES), SUBLANES)
            src = zbuf if r == 0 else zsh.at[r - 1]
            win = src[pl.ds(start, CONV_ROWS), :].reshape(groups, SUBLANES, cw_width)
            acc = acc + win * cw_ref[k][None]
        conv_scr[pl.ds(pl.multiple_of(base, CONV_ROWS), CONV_ROWS), :] = acc.reshape(CONV_ROWS, cw_width)
        return carry

    lax.fori_loop(0, t // CONV_ROWS, conv_block, 0)
    conv = conv_scr[...]
    mu = jnp.mean(conv, axis=-1, keepdims=True)
    cen = conv - mu
    var = jnp.mean(cen * cen, axis=-1, keepdims=True)
    zn = cen * lax.rsqrt(var + NORM_EPS) * lg_ref[...] + lb_ref[...]
    ycat[:, pw_width:pw_width + cw_width] = (zn * jax.nn.sigmoid(zn)).astype(ycat.dtype)

    o_ref[0] = x + _mm(ycat[...], wout_ref[...])


def _ab_mixer(x, g, win, pw, ps, cw, cb, lg, lb, wout):
    b, s, d = x.shape
    t = MIX_TOKENS
    pwid, cwid = ps.shape[1], cb.shape[1]
    return pl.pallas_call(
        _ab_kernel,
        grid=(b, s // t),
        in_specs=[
            pl.BlockSpec((1, t, d), lambda bi, j: (bi, j, 0)),
            _full((1, d)), _full(win.shape), _full(pw.shape), _full(ps.shape), _full(cw.shape),
            _full(cb.shape), _full(lg.shape), _full(lb.shape), _full(wout.shape),
        ],
        out_specs=pl.BlockSpec((1, t, d), lambda bi, j: (bi, j, 0)),
        out_shape=jax.ShapeDtypeStruct((b, s, d), F32),
        scratch_shapes=[
            pltpu.VMEM((HALO + t, pwid), F32),
            pltpu.VMEM((HALO + t, cwid), F32),
            pltpu.VMEM((SUBLANES - 1, HALO + t, cwid), F32),
            pltpu.VMEM((t, pwid + cwid), _MXU_DTYPE),
            pltpu.VMEM((t, cwid), F32),
        ],
        compiler_params=_params(),
        name="ab_mixer",
    )(x, g, win, pw, ps, cw, cb, lg, lb, wout)


def _cin_kernel(x_ref, g_ref, wn_ref, wt_ref, ww_ref, k_out, ki_out, qt_out, vt_out, qit_out, wt_out):
    t = x_ref.shape[1]
    aw = qt_out.shape[1]
    iw = qit_out.shape[1]
    h = _rms(x_ref[0], g_ref[...]).astype(_MXU_DTYPE)
    nat = _mm(h, wn_ref[...])
    for p in range(aw // LANES):
        k_out[0, p] = nat[:, p * LANES:(p + 1) * LANES].astype(k_out.dtype)
    ki_out[0] = nat[:, aw:aw + IDX_DIM].astype(ki_out.dtype)
    tr = _mm_nt(wt_ref[...], h)
    qt_out[0] = (tr[0:aw] * (HEAD_DIM ** -0.5 * LOG2E)).astype(qt_out.dtype)
    vt_out[0] = tr[aw:2 * aw].astype(vt_out.dtype)
    qit_out[0] = tr[2 * aw:2 * aw + iw].astype(qit_out.dtype)
    wt_out[0] = _mm_nt(ww_ref[...], h)


def _c_inproj(x, g, wn, wt, ww):
    b, s, d = x.shape
    t = MIX_TOKENS
    aw = N_HEADS * HEAD_DIM
    iw = N_IDX_HEADS * IDX_DIM
    md = _MXU_DTYPE
    return pl.pallas_call(
        _cin_kernel,
        grid=(b, s // t),
        in_specs=[
            pl.BlockSpec((1, t, d), lambda bi, j: (bi, j, 0)),
            _full((1, d)), _full(wn.shape), _full(wt.shape), _full(ww.shape),
        ],
        out_specs=[
            pl.BlockSpec((1, aw // LANES, t, LANES), lambda bi, j: (bi, 0, j, 0)),
            pl.BlockSpec((1, t, IDX_DIM), lambda bi, j: (bi, j, 0)),
            pl.BlockSpec((1, aw, t), lambda bi, j: (bi, 0, j)),
            pl.BlockSpec((1, aw, t), lambda bi, j: (bi, 0, j)),
            pl.BlockSpec((1, iw, t), lambda bi, j: (bi, 0, j)),
            pl.BlockSpec((1, N_IDX_HEADS, t), lambda bi, j: (bi, 0, j)),
        ],
        out_shape=[
            jax.ShapeDtypeStruct((b, aw // LANES, s, LANES), md),
            jax.ShapeDtypeStruct((b, s, IDX_DIM), md),
            jax.ShapeDtypeStruct((b, aw, s), md),
            jax.ShapeDtypeStruct((b, aw, s), md),
            jax.ShapeDtypeStruct((b, iw, s), md),
            jax.ShapeDtypeStruct((b, N_IDX_HEADS, s), F32),
        ],
        compiler_params=_params(),
        name="c_inproj",
    )(x, g, wn, wt, ww)


def _bias_kernel(rel_ref, bkt_ref, o_ref):
    h = pl.program_id(1)
    bkt = bkt_ref[0]
    acc = jnp.zeros(bkt.shape, F32)
    for b in range(REL_BUCKETS):
        acc = jnp.where(bkt == b, rel_ref[b, h], acc)
    o_ref[0, 0] = acc * LOG2E


def _bias_tiles(rel_bias, bkt):
    nd = bkt.shape[0]
    return pl.pallas_call(
        _bias_kernel,
        grid=(nd, N_HEADS),
        in_specs=[
            pl.BlockSpec(memory_space=pltpu.SMEM),
            pl.BlockSpec((1, KC, TQ), lambda dl, h: (dl, 0, 0)),
        ],
        out_specs=pl.BlockSpec((1, 1, KC, TQ), lambda dl, h: (dl, h, 0, 0)),
        out_shape=jax.ShapeDtypeStruct((nd, N_HEADS, KC, TQ), F32),
        name="bias_tiles",
    )(rel_bias, bkt)


def _t5_bucket(dist):
    max_exact = REL_BUCKETS // 2
    d = jnp.maximum(dist, max_exact).astype(F32)
    large = max_exact + (jnp.log(d / max_exact) / math.log(REL_MAX_DIST / max_exact)
                         * (REL_BUCKETS - max_exact)).astype(I32)
    large = jnp.minimum(large, REL_BUCKETS - 1)
    return jnp.where(dist < max_exact, dist, large)


def _colsum8(v):
    return v.reshape(v.shape[0] // SUBLANES, SUBLANES, v.shape[1]).sum(axis=0)


def _colmax8(v):
    return v.reshape(v.shape[0] // SUBLANES, SUBLANES, v.shape[1]).max(axis=0)


def _dsa_block(n, far_ref, qt_ref, qit_ref, wt_ref, k_ref, ki_ref, vt_ref, tile_ref,
               key_scr, hi_scr, lo_scr, madd_scr, logit_scr, ot_scr, *, top_k):
    idx_scale = (N_IDX_HEADS * IDX_DIM) ** -0.5
    causal = lax.broadcasted_iota(I32, (KC, TQ), 0) <= lax.broadcasted_iota(I32, (KC, TQ), 1)
    rows = lambda c: slice(c * KC, (c + 1) * KC)

    for c in range(n):
        ki_c = ki_ref[0, rows(c), :]
        acc = jnp.zeros((KC, TQ), F32)
        for h in range(N_IDX_HEADS):
            r = _mm(ki_c, qit_ref[0, h * IDX_DIM:(h + 1) * IDX_DIM, :])
            acc = acc + jnp.maximum(r, 0.0) * wt_ref[0, h:h + 1, :]
        bits = lax.bitcast_convert_type(acc * idx_scale, I32)
        key = jnp.where(bits < 0, bits ^ 0x7FFFFFFF, bits)
        if c == n - 1:
            key = jnp.where(causal, key, INT_MIN)
        key_scr[rows(c), :] = key
        hi_scr[rows(c), :] = (key >> 16).astype(I16)

    if n * KC <= top_k:
        for c in range(n):
            madd_scr[rows(c), :] = jnp.where(key_scr[rows(c), :] != INT_MIN, 0.0, -jnp.inf)
    else:
        def count16(ref, pred, t16):
            parts = []
            for c in range(n):
                hit = pred(ref[rows(c), :], t16).astype(I16)
                parts += [hit[r * PACK16:(r + 1) * PACK16] for r in range(KC // PACK16)]
            while len(parts) > 1:
                parts = [a + b for a, b in zip(parts[0::2], parts[1::2])] + parts[len(parts) & ~1:]
            return parts[0].astype(I32).sum(axis=0, keepdims=True)

        def search16(ref, want):
            def bit_body(it, t_u):
                cand = t_u | lax.shift_left(jnp.int32(1), 15 - it)
                cnt = count16(ref, lambda v, t_: v >= t_, (cand - HALF16).astype(I16))
                return jnp.where(cnt >= want, cand, t_u)
            return lax.fori_loop(0, 16, bit_body, jnp.zeros((1, TQ), I32)) - HALF16

        t_hi = search16(hi_scr, top_k)
        above = count16(hi_scr, lambda v, t_: v > t_, t_hi.astype(I16))
        for c in range(n):
            k = key_scr[rows(c), :]
            lo = jnp.where((k >> 16) == t_hi, (k & 0xFFFF) - HALF16, -HALF16)
            lo_scr[rows(c), :] = lo.astype(I16)
        t_lo = search16(lo_scr, top_k - above)
        thr = lax.shift_left(t_hi, 16) | (t_lo + HALF16)

        def count(pred, t_):
            acc = jnp.zeros((SUBLANES, TQ), I32)
            for c in range(n):
                acc = acc + _colsum8(pred(key_scr[rows(c), :], t_).astype(I32))
            return acc.sum(axis=0, keepdims=True)

        need = (top_k - count(lambda k, t_: k > t_, thr)).astype(F32)
        tri = (lax.broadcasted_iota(I32, (KC, KC), 1)
               < lax.broadcasted_iota(I32, (KC, KC), 0)).astype(_MXU_DTYPE)
        seen = jnp.zeros((1, TQ), F32)
        for c in range(n):
            k = key_scr[rows(c), :]
            eq = (k == thr).astype(F32)
            before = _mm(tri, eq.astype(_MXU_DTYPE)) + seen
            sel = ((k > thr) | ((eq > 0.0) & (before < need))) & (k != INT_MIN)
            madd_scr[rows(c), :] = jnp.where(sel, 0.0, -jnp.inf)
            seen = seen + _colsum8(eq).sum(axis=0, keepdims=True)

    pair_row = lax.broadcasted_iota(I32, (2 * HEAD_DIM, TQ), 0)
    ones_rows = jnp.ones((ONES_ROWS, KC), _MXU_DTYPE)
    n_far = max(n - 2, 0)

    def scores(p, slot):
        prow = pl.ds(pl.multiple_of(p * 2 * HEAD_DIM, 2 * HEAD_DIM), 2 * HEAD_DIM)
        qt = qt_ref[0, prow, :]
        zero = jnp.zeros_like(qt)
        q2 = jnp.concatenate([jnp.where(pair_row < HEAD_DIM, qt, zero),
                              jnp.where(pair_row >= HEAD_DIM, qt, zero)], axis=1)
        fars = [far_ref[2 * p + half] * LOG2E for half in range(2)]
        m_far = [jnp.full((SUBLANES, TQ), -jnp.inf, F32)] * 2
        m_near = list(m_far)
        for c in range(n):
            s2 = _mm(k_ref[0, p, rows(c), :], q2)
            for half in range(2):
                s = s2[:, half * TQ:(half + 1) * TQ] + madd_scr[rows(c), :]
                if c < n_far:
                    m_far[half] = jnp.maximum(m_far[half], _colmax8(s))
                else:
                    s = s + tile_ref[n - 1 - c, 2 * p + half]
                    m_near[half] = jnp.maximum(m_near[half], _colmax8(s))
                logit_scr[2 * slot + half, rows(c), :] = s
        return [jnp.maximum(m_far[half] + fars[half], m_near[half]).max(axis=0, keepdims=True)
                for half in range(2)]

    def weighted_values(p, slot, m):
        prow = pl.ds(pl.multiple_of(p * 2 * HEAD_DIM, 2 * HEAD_DIM), 2 * HEAD_DIM)
        fars = [far_ref[2 * p + half] * LOG2E for half in range(2)]
        acc = jnp.zeros((2 * HEAD_DIM + ONES_ROWS, 2 * TQ), F32)
        for c in range(n):
            e2 = jnp.concatenate(
                [jnp.exp2(logit_scr[2 * slot + half, rows(c), :]
                          - (m[half] - fars[half] if c < n_far else m[half])).astype(_MXU_DTYPE)
                 for half in range(2)], axis=1)
            lhs = jnp.concatenate([vt_ref[0, prow, rows(c)], ones_rows], axis=0)
            acc = acc + _mm(lhs, e2)
        for half in range(2):
            cols = slice(half * TQ, (half + 1) * TQ)
            ot_scr[pl.ds(pl.multiple_of((2 * p + half) * HEAD_DIM, HEAD_DIM), HEAD_DIM), :] = (
                acc[half * HEAD_DIM:(half + 1) * HEAD_DIM, cols]
                / acc[2 * HEAD_DIM:2 * HEAD_DIM + 1, cols])

    def group_body(g, carry):
        ms = [scores(g * PAIRS_PER_STEP + u, u) for u in range(PAIRS_PER_STEP)]
        for u in range(PAIRS_PER_STEP):
            weighted_values(g * PAIRS_PER_STEP + u, u, ms[u])
        return carry

    lax.fori_loop(0, N_HEADS // 2 // PAIRS_PER_STEP, group_body, 0)


def _dsa_kernel(far_ref, x_ref, qt_ref, qit_ref, wt_ref, k_ref, ki_ref, vt_ref, tile_ref, wout_ref,
                o_ref, key_scr, hi_scr, lo_scr, madd_scr, logit_scr, ot_scr, *, top_k):
    i = pl.program_id(1)
    for n in range(1, k_ref.shape[2] // KC + 1):
        pl.when(i == n - 1)(functools.partial(
            _dsa_block, n, far_ref, qt_ref, qit_ref, wt_ref, k_ref, ki_ref, vt_ref, tile_ref,
            key_scr, hi_scr, lo_scr, madd_scr, logit_scr, ot_scr, top_k=top_k))
    o_ref[0] = x_ref[0] + _mm_tn(ot_scr[...].astype(_MXU_DTYPE), wout_ref[...])


def _dsa(x, far, qt, qit, wt, k, ki, vt, tiles, wout, top_k):
    b, s, d = x.shape
    aw = N_HEADS * HEAD_DIM
    iw = N_IDX_HEADS * IDX_DIM
    return pl.pallas_call(
        functools.partial(_dsa_kernel, top_k=top_k),
        grid=(b, s // TQ),
        in_specs=[
            pl.BlockSpec(memory_space=pltpu.SMEM),
            pl.BlockSpec((1, TQ, d), lambda bi, i: (bi, i, 0)),
            pl.BlockSpec((1, aw, TQ), lambda bi, i: (bi, 0, i)),
            pl.BlockSpec((1, iw, TQ), lambda bi, i: (bi, 0, i)),
            pl.BlockSpec((1, N_IDX_HEADS, TQ), lambda bi, i: (bi, 0, i)),
            pl.BlockSpec((1, aw // LANES, s, LANES), lambda bi, i: (bi, 0, 0, 0)),
            pl.BlockSpec((1, s, IDX_DIM), lambda bi, i: (bi, 0, 0)),
            pl.BlockSpec((1, aw, s), lambda bi, i: (bi, 0, 0)),
            _full(tiles.shape), _full(wout.shape),
        ],
        out_specs=pl.BlockSpec((1, TQ, d), lambda bi, i: (bi, i, 0)),
        out_shape=jax.ShapeDtypeStruct((b, s, d), F32),
        scratch_shapes=[
            pltpu.VMEM((s, TQ), I32),
            pltpu.VMEM((s, TQ), I16),
            pltpu.VMEM((s, TQ), I16),
            pltpu.VMEM((s, TQ), F32),
            pltpu.VMEM((2 * PAIRS_PER_STEP, s, TQ), F32),
            pltpu.VMEM((aw, TQ), F32),
        ],
        compiler_params=_params(),
        name="dsa",
    )(far, x, qt, qit, wt, k, ki, vt, tiles, wout)


def kernel(x, ffn1_norm, ffn1_w_gate, ffn1_w_up, ffn1_w_down, mix_norm, ffn2_norm, ffn2_w_gate,
           ffn2_w_up, ffn2_w_down, ab_w_in, pool_w, pool_scale, conv_w, conv_b, conv_ln_g,
           conv_ln_b, ab_w_out, c_w_in, c_w_out, rel_bias, final_norm):
    b, s, d = x.shape
    depth = ffn1_norm.shape[0]
    md = _MXU_DTYPE
    aw = N_HEADS * HEAD_DIM
    iw = N_IDX_HEADS * IDX_DIM
    top_k = min(TOPK_MAX, s // 4)
    assert s % MIX_TOKENS == 0 and (b * s) % FFN_TOKENS == 0 and s % TQ == 0 and top_k <= KC

    row = lambda v: v.reshape(1, -1)

    if depth > 1:
        sk = jnp.arange(KC, dtype=I32)[:, None]
        tq = jnp.arange(TQ, dtype=I32)[None, :]
        bkt = jnp.stack([_t5_bucket(jnp.maximum(tq + dl * TQ - sk, 0)) for dl in (0, 1)])
        tiles = _bias_tiles(rel_bias, bkt)
        assert 2 * TQ - KC + 1 >= REL_MAX_DIST
        far = rel_bias[REL_BUCKETS - 1]

    for layer in range(depth):
        i = layer // 2
        last = layer == depth - 1
        x2 = _ffn(x.reshape(b * s, d), row(ffn1_norm[layer]), ffn1_w_gate[layer].astype(md),
                  ffn1_w_up[layer].astype(md), ffn1_w_down[layer].astype(md), row(final_norm), False)
        x = x2.reshape(b, s, d)
        if layer % 2 == 0:
            x = _ab_mixer(x, row(mix_norm[layer]), ab_w_in[i].astype(md), pool_w[i].astype(md),
                          row(pool_scale[i]),
                          jnp.broadcast_to(conv_w[i][:, None, :], (CONV_KERNEL, SUBLANES, conv_w.shape[-1])),
                          row(conv_b[i]), row(conv_ln_g[i]),
                          row(conv_ln_b[i]), ab_w_out[i].astype(md))
        else:
            w = c_w_in[i]
            wq, wk, wv = w[:, 0:aw], w[:, aw:2 * aw], w[:, 2 * aw:3 * aw]
            wqi = w[:, 3 * aw:3 * aw + iw]
            wki = w[:, 3 * aw + iw:3 * aw + iw + IDX_DIM]
            www = w[:, 3 * aw + iw + IDX_DIM:]
            wn = jnp.concatenate([wk, wki], axis=1).astype(md)
            wt = jnp.concatenate([wq, wv, wqi], axis=1).T.astype(md)
            k, ki, qt, vt, qit, wts = _c_inproj(x, row(mix_norm[layer]), wn, wt, www.T.astype(md))
            x = _dsa(x, far, qt, qit, wts, k, ki, vt, tiles, c_w_out[i].astype(md), top_k)
        x2 = _ffn(x.reshape(b * s, d), row(ffn2_norm[layer]), ffn2_w_gate[layer].astype(md),
                  ffn2_w_up[layer].astype(md), ffn2_w_down[layer].astype(md), row(final_norm), last)
        x = x2.reshape(b, s, d)
    return x
```

```python
import functools
import math

import jax
import jax.numpy as jnp
from jax import lax
from jax.experimental import pallas as pl
from jax.experimental.pallas import tpu as pltpu

F32 = jnp.float32
I32 = jnp.int32
I16 = jnp.int16
_MXU_DTYPE = jnp.bfloat16

NORM_EPS = 1e-6
FFN_RES = 0.5
POOL_WINDOWS = (2, 4, 8, 16)
POOL_GROUP = 128
CONV_KERNEL = 31
N_HEADS = 16
HEAD_DIM = 64
N_IDX_HEADS = 8
IDX_DIM = 64
TOPK_MAX = 256
REL_BUCKETS = 32
REL_MAX_DIST = 128

V7X_VMEM_BYTES = 64 * 1024 * 1024
VMEM_LIMIT = V7X_VMEM_BYTES - 6 * 1024 * 1024
SUBLANES = 8
LANES = 128

FFN_TOKENS = 512
MIX_TOKENS = 512
CONV_ROWS = 32
HALO = 32
TQ = 256
KC = 256
INT_MIN = -2 ** 31
HALF16 = 2 ** 15
PACK16 = 16
LOG2E = 1.0 / math.log(2.0)
PAIRS_PER_STEP = 2
ONES_ROWS = 16


def _rms(x, g):
    return x * lax.rsqrt(jnp.mean(x * x, axis=-1, keepdims=True) + NORM_EPS) * g


def _mm(a, b):
    return jnp.dot(a, b, preferred_element_type=F32)


def _mm_nt(a, b):
    return lax.dot_general(a, b, (((1,), (1,)), ((), ())), preferred_element_type=F32)


def _mm_tn(a, b):
    return lax.dot_general(a, b, (((0,), (0,)), ((), ())), preferred_element_type=F32)


def _params():
    return pltpu.CompilerParams(vmem_limit_bytes=VMEM_LIMIT)


def _full(shape):
    n = len(shape)
    return pl.BlockSpec(shape, lambda *_: (0,) * n)


def _ffn_kernel(x_ref, g_ref, wg_ref, wu_ref, wd_ref, fg_ref, o_ref, *, final_norm):
    x = x_ref[...]
    h = _rms(x, g_ref[...]).astype(_MXU_DTYPE)
    gate = _mm(h, wg_ref[...])
    up = _mm(h, wu_ref[...])
    act = (gate * jax.nn.sigmoid(gate) * up).astype(_MXU_DTYPE)
    y = x + FFN_RES * _mm(act, wd_ref[...])
    if final_norm:
        y = _rms(y, fg_ref[...])
    o_ref[...] = y


def _ffn(x2, g, wg, wu, wd, fg, final_norm):
    n, d = x2.shape
    f = wg.shape[1]
    t = FFN_TOKENS
    return pl.pallas_call(
        functools.partial(_ffn_kernel, final_norm=final_norm),
        grid=(n // t,),
        in_specs=[
            pl.BlockSpec((t, d), lambda i: (i, 0)),
            _full((1, d)), _full((d, f)), _full((d, f)), _full((f, d)), _full((1, d)),
        ],
        out_specs=pl.BlockSpec((t, d), lambda i: (i, 0)),
        out_shape=jax.ShapeDtypeStruct((n, d), F32),
        compiler_params=_params(),
        name="ffn_final" if final_norm else "ffn",
    )(x2, g, wg, wu, wd, fg)


def _ab_kernel(x_ref, g_ref, win_ref, pw_ref, ps_ref, cw_ref, cb_ref, lg_ref, lb_ref, wout_ref,
               o_ref, ubuf, zbuf, zsh, ycat, conv_scr):
    t = x_ref.shape[1]
    pw_width = ps_ref.shape[1]
    cw_width = cb_ref.shape[1]
    j = pl.program_id(1)

    @pl.when(j == 0)
    def _():
        ubuf[0:HALO, :] = jnp.zeros((HALO, pw_width), F32)
        zbuf[0:HALO, :] = jnp.zeros((HALO, cw_width), F32)

    @pl.when(j > 0)
    def _():
        ubuf[0:HALO, :] = ubuf[t:t + HALO, :]
        zbuf[0:HALO, :] = zbuf[t:t + HALO, :]

    x = x_ref[0]
    h = _rms(x, g_ref[...]).astype(_MXU_DTYPE)
    proj = _mm(h, win_ref[...])
    u = proj[:, 0:pw_width]
    val = proj[:, pw_width:pw_width + cw_width]
    gate = proj[:, pw_width + cw_width:pw_width + 2 * cw_width]
    ubuf[HALO:HALO + t, :] = u
    zbuf[HALO:HALO + t, :] = val * jax.nn.sigmoid(gate)
    for r in range(1, SUBLANES):
        zsh[r - 1, SUBLANES:HALO + t, :] = zbuf[SUBLANES - r:HALO + t - r, :]

    pos = j * t + lax.broadcasted_iota(I32, (t, POOL_GROUP), 0)
    for g, w in enumerate(POOL_WINDOWS):
        lo, hi = g * POOL_GROUP, (g + 1) * POOL_GROUP
        ug = ubuf[HALO:HALO + t, lo:hi]
        acc = ug
        for back in range(1, w):
            acc = acc + ubuf[HALO - back:HALO - back + t, lo:hi]
        count = jnp.minimum(pos + 1, w).astype(F32)
        diff = (acc / count - ug).astype(_MXU_DTYPE)
        ycat[:, lo:hi] = (_mm(diff, pw_ref[g]) * ps_ref[:, lo:hi]).astype(ycat.dtype)

    def conv_block(rb, carry):
        base = rb * CONV_ROWS
        groups = CONV_ROWS // SUBLANES
        acc = jnp.broadcast_to(cb_ref[...].reshape(1, 1, cw_width), (groups, SUBLANES, cw_width))
        for k in range(CONV_KERNEL):
            a, r = divmod(CONV_KERNEL - 1 - k, SUBLANES)
            start = pl.multiple_of(base + (HALO - a * SUBLANES), SUBLANES)
            src = zbuf if r == 0 else zsh.at[r - 1]
            win = src[pl.ds(start, CONV_ROWS), :].reshape(groups, SUBLANES, cw_width)
            acc = acc + win * cw_ref[k][None]
        conv_scr[pl.ds(pl.multiple_of(base, CONV_ROWS), CONV_ROWS), :] = acc.reshape(CONV_ROWS, cw_width)
        return carry

    lax.fori_loop(0, t // CONV_ROWS, conv_block, 0)
    conv = conv_scr[...]
    mu = jnp.mean(conv, axis=-1, keepdims=True)
    cen = conv - mu
    var = jnp.mean(cen * cen, axis=-1, keepdims=True)
    zn = cen * lax.rsqrt(var + NORM_EPS) * lg_ref[...] + lb_ref[...]
    ycat[:, pw_width:pw_width + cw_width] = (zn * jax.nn.sigmoid(zn)).astype(ycat.dtype)

    o_ref[0] = x + _mm(ycat[...], wout_ref[...])


def _ab_mixer(x, g, win, pw, ps, cw, cb, lg, lb, wout):
    b, s, d = x.shape
    t = MIX_TOKENS
    pwid, cwid = ps.shape[1], cb.shape[1]
    return pl.pallas_call(
        _ab_kernel,
        grid=(b, s // t),
        in_specs=[
            pl.BlockSpec((1, t, d), lambda bi, j: (bi, j, 0)),
            _full((1, d)), _full(win.shape), _full(pw.shape), _full(ps.shape), _full(cw.shape),
            _full(cb.shape), _full(lg.shape), _full(lb.shape), _full(wout.shape),
        ],
        out_specs=pl.BlockSpec((1, t, d), lambda bi, j: (bi, j, 0)),
        out_shape=jax.ShapeDtypeStruct((b, s, d), F32),
        scratch_shapes=[
            pltpu.VMEM((HALO + t, pwid), F32),
            pltpu.VMEM((HALO + t, cwid), F32),
            pltpu.VMEM((SUBLANES - 1, HALO + t, cwid), F32),
            pltpu.VMEM((t, pwid + cwid), _MXU_DTYPE),
            pltpu.VMEM((t, cwid), F32),
        ],
        compiler_params=_params(),
        name="ab_mixer",
    )(x, g, win, pw, ps, cw, cb, lg, lb, wout)


def _cin_kernel(x_ref, g_ref, wn_ref, wt_ref, ww_ref, k_out, ki_out, qt_out, vt_out, qit_out, wt_out):
    t = x_ref.shape[1]
    aw = qt_out.shape[1]
    iw = qit_out.shape[1]
    h = _rms(x_ref[0], g_ref[...]).astype(_MXU_DTYPE)
    nat = _mm(h, wn_ref[...])
    for p in range(aw // LANES):
        k_out[0, p] = nat[:, p * LANES:(p + 1) * LANES].astype(k_out.dtype)
    ki_out[0] = nat[:, aw:aw + IDX_DIM].astype(ki_out.dtype)
    tr = _mm_nt(wt_ref[...], h)
    qt_out[0] = (tr[0:aw] * (HEAD_DIM ** -0.5 * LOG2E)).astype(qt_out.dtype)
    vt_out[0] = tr[aw:2 * aw].astype(vt_out.dtype)
    qit_out[0] = tr[2 * aw:2 * aw + iw].astype(qit_out.dtype)
    wt_out[0] = _mm_nt(ww_ref[...], h)


def _c_inproj(x, g, wn, wt, ww):
    b, s, d = x.shape
    t = MIX_TOKENS
    aw = N_HEADS * HEAD_DIM
    iw = N_IDX_HEADS * IDX_DIM
    md = _MXU_DTYPE
    return pl.pallas_call(
        _cin_kernel,
        grid=(b, s // t),
        in_specs=[
            pl.BlockSpec((1, t, d), lambda bi, j: (bi, j, 0)),
            _full((1, d)), _full(wn.shape), _full(wt.shape), _full(ww.shape),
        ],
        out_specs=[
            pl.BlockSpec((1, aw // LANES, t, LANES), lambda bi, j: (bi, 0, j, 0)),
            pl.BlockSpec((1, t, IDX_DIM), lambda bi, j: (bi, j, 0)),
            pl.BlockSpec((1, aw, t), lambda bi, j: (bi, 0, j)),
            pl.BlockSpec((1, aw, t), lambda bi, j: (bi, 0, j)),
            pl.BlockSpec((1, iw, t), lambda bi, j: (bi, 0, j)),
            pl.BlockSpec((1, N_IDX_HEADS, t), lambda bi, j: (bi, 0, j)),
        ],
        out_shape=[
            jax.ShapeDtypeStruct((b, aw // LANES, s, LANES), md),
            jax.ShapeDtypeStruct((b, s, IDX_DIM), md),
            jax.ShapeDtypeStruct((b, aw, s), md),
            jax.ShapeDtypeStruct((b, aw, s), md),
            jax.ShapeDtypeStruct((b, iw, s), md),
            jax.ShapeDtypeStruct((b, N_IDX_HEADS, s), F32),
        ],
        compiler_params=_params(),
        name="c_inproj",
    )(x, g, wn, wt, ww)


def _bias_kernel(rel_ref, bkt_ref, o_ref):
    h = pl.program_id(1)
    bkt = bkt_ref[0]
    acc = jnp.zeros(bkt.shape, F32)
    for b in range(REL_BUCKETS):
        acc = jnp.where(bkt == b, rel_ref[b, h], acc)
    o_ref[0, 0] = acc * LOG2E


def _bias_tiles(rel_bias, bkt):
    nd = bkt.shape[0]
    return pl.pallas_call(
        _bias_kernel,
        grid=(nd, N_HEADS),
        in_specs=[
            pl.BlockSpec(memory_space=pltpu.SMEM),
            pl.BlockSpec((1, KC, TQ), lambda dl, h: (dl, 0, 0)),
        ],
        out_specs=pl.BlockSpec((1, 1, KC, TQ), lambda dl, h: (dl, h, 0, 0)),
        out_shape=jax.ShapeDtypeStruct((nd, N_HEADS, KC, TQ), F32),
        name="bias_tiles",
    )(rel_bias, bkt)


def _t5_bucket(dist):
    max_exact = REL_BUCKETS // 2
    d = jnp.maximum(dist, max_exact).astype(F32)
    large = max_exact + (jnp.log(d / max_exact) / math.log(REL_MAX_DIST / max_exact)
                         * (REL_BUCKETS - max_exact)).astype(I32)
    large = jnp.minimum(large, REL_BUCKETS - 1)
    return jnp.where(dist < max_exact, dist, large)


def _colsum8(v):
    return v.reshape(v.shape[0] // SUBLANES, SUBLANES, v.shape[1]).sum(axis=0)


def _colmax8(v):
    return v.reshape(v.shape[0] // SUBLANES, SUBLANES, v.shape[1]).max(axis=0)


def _dsa_block(n, far_ref, qt_ref, qit_ref, wt_ref, k_ref, ki_ref, vt_ref, tile_ref,
               key_scr, hi_scr, lo_scr, madd_scr, logit_scr, ot_scr, *, top_k):
    idx_scale = (N_IDX_HEADS * IDX_DIM) ** -0.5
    rows = lambda c: slice(c * KC, (c + 1) * KC)

    def chunk(c):
        return pl.ds(pl.multiple_of(c * KC, KC), KC)

    def over_chunks(body, init):
        return lax.fori_loop(0, n, body, init)

    row_id = lax.broadcasted_iota(I32, (KC, TQ), 0)
    q_pos = lax.broadcasted_iota(I32, (KC, TQ), 1) + (n - 1) * KC

    def key_body(c, carry):
        ki_c = ki_ref[0, chunk(c), :]
        acc = jnp.zeros((KC, TQ), F32)
        for h in range(N_IDX_HEADS):
            r = _mm(ki_c, qit_ref[0, h * IDX_DIM:(h + 1) * IDX_DIM, :])
            acc = acc + jnp.maximum(r, 0.0) * wt_ref[0, h:h + 1, :]
        bits = lax.bitcast_convert_type(acc * idx_scale, I32)
        key = jnp.where(bits < 0, bits ^ 0x7FFFFFFF, bits)
        key = jnp.where(row_id + c * KC <= q_pos, key, INT_MIN)
        key_scr[chunk(c), :] = key
        hi_scr[chunk(c), :] = (key >> 16).astype(I16)
        return carry

    over_chunks(key_body, 0)

    if n * KC <= top_k:
        def all_body(c, carry):
            madd_scr[chunk(c), :] = jnp.where(key_scr[chunk(c), :] != INT_MIN, 0.0, -jnp.inf)
            return carry
        over_chunks(all_body, 0)
    else:
        def count16(ref, pred, t16):
            parts = []
            for c in range(n):
                hit = pred(ref[rows(c), :], t16).astype(I16)
                parts += [hit[r * PACK16:(r + 1) * PACK16] for r in range(KC // PACK16)]
            while len(parts) > 1:
                parts = [a + b for a, b in zip(parts[0::2], parts[1::2])] + parts[len(parts) & ~1:]
            return parts[0].astype(I32).sum(axis=0, keepdims=True)

        def search16(ref, want):
            def bit_body(it, t_u):
                cand = t_u | lax.shift_left(jnp.int32(1), 15 - it)
                cnt = count16(ref, lambda v, t_: v >= t_, (cand - HALF16).astype(I16))
                return jnp.where(cnt >= want, cand, t_u)
            return lax.fori_loop(0, 16, bit_body, jnp.zeros((1, TQ), I32)) - HALF16

        t_hi = search16(hi_scr, top_k)
        above = count16(hi_scr, lambda v, t_: v > t_, t_hi.astype(I16))

        def lo_body(c, carry):
            k = key_scr[chunk(c), :]
            lo = jnp.where((k >> 16) == t_hi, (k & 0xFFFF) - HALF16, -HALF16)
            lo_scr[chunk(c), :] = lo.astype(I16)
            return carry

        over_chunks(lo_body, 0)
        t_lo = search16(lo_scr, top_k - above)
        thr = lax.shift_left(t_hi, 16) | (t_lo + HALF16)

        n_gt = over_chunks(lambda c, a: a + _colsum8((key_scr[chunk(c), :] > thr).astype(I32)),
                           jnp.zeros((SUBLANES, TQ), I32))
        need = (top_k - n_gt.sum(axis=0, keepdims=True)).astype(F32)
        tri = (lax.broadcasted_iota(I32, (KC, KC), 1)
               < lax.broadcasted_iota(I32, (KC, KC), 0)).astype(_MXU_DTYPE)

        def mask_body(c, seen):
            k = key_scr[chunk(c), :]
            eq = (k == thr).astype(F32)
            before = _mm(tri, eq.astype(_MXU_DTYPE)) + seen
            sel = ((k > thr) | ((eq > 0.0) & (before < need))) & (k != INT_MIN)
            madd_scr[chunk(c), :] = jnp.where(sel, 0.0, -jnp.inf)
            return seen + _colsum8(eq).sum(axis=0, keepdims=True)

        over_chunks(mask_body, jnp.zeros((1, TQ), F32))

    pair_row = lax.broadcasted_iota(I32, (2 * HEAD_DIM, TQ), 0)
    ones_rows = jnp.ones((ONES_ROWS, KC), _MXU_DTYPE)
    n_far = max(n - 2, 0)

    def scores(p, slot):
        prow = pl.ds(pl.multiple_of(p * 2 * HEAD_DIM, 2 * HEAD_DIM), 2 * HEAD_DIM)
        qt = qt_ref[0, prow, :]
        zero = jnp.zeros_like(qt)
        q2 = jnp.concatenate([jnp.where(pair_row < HEAD_DIM, qt, zero),
                              jnp.where(pair_row >= HEAD_DIM, qt, zero)], axis=1)
        fars = [far_ref[2 * p + half] * LOG2E for half in range(2)]
        m_far = [jnp.full((SUBLANES, TQ), -jnp.inf, F32)] * 2
        m_near = list(m_far)
        for c in range(n):
            s2 = _mm(k_ref[0, p, rows(c), :], q2)
            for half in range(2):
                s = s2[:, half * TQ:(half + 1) * TQ] + madd_scr[rows(c), :]
                if c < n_far:
                    m_far[half] = jnp.maximum(m_far[half], _colmax8(s))
                else:
                    s = s + tile_ref[n - 1 - c, 2 * p + half]
                    m_near[half] = jnp.maximum(m_near[half], _colmax8(s))
                logit_scr[2 * slot + half, rows(c), :] = s
        return [jnp.maximum(m_far[half] + fars[half], m_near[half]).max(axis=0, keepdims=True)
                for half in range(2)]

    def weighted_values(p, slot, m):
        prow = pl.ds(pl.multiple_of(p * 2 * HEAD_DIM, 2 * HEAD_DIM), 2 * HEAD_DIM)
        fars = [far_ref[2 * p + half] * LOG2E for half in range(2)]
        acc = jnp.zeros((2 * HEAD_DIM + ONES_ROWS, 2 * TQ), F32)
        for c in range(n):
            e2 = jnp.concatenate(
                [jnp.exp2(logit_scr[2 * slot + half, rows(c), :]
                          - (m[half] - fars[half] if c < n_far else m[half])).astype(_MXU_DTYPE)
                 for half in range(2)], axis=1)
            lhs = jnp.concatenate([vt_ref[0, prow, rows(c)], ones_rows], axis=0)
            acc = acc + _mm(lhs, e2)
        for half in range(2):
            cols = slice(half * TQ, (half + 1) * TQ)
            ot_scr[pl.ds(pl.multiple_of((2 * p + half) * HEAD_DIM, HEAD_DIM), HEAD_DIM), :] = (
                acc[half * HEAD_DIM:(half + 1) * HEAD_DIM, cols]
                / acc[2 * HEAD_DIM:2 * HEAD_DIM + 1, cols])

    def group_body(g, carry):
        ms = [scores(g * PAIRS_PER_STEP + u, u) for u in range(PAIRS_PER_STEP)]
        for u in range(PAIRS_PER_STEP):
            weighted_values(g * PAIRS_PER_STEP + u, u, ms[u])
        return carry

    lax.fori_loop(0, N_HEADS // 2 // PAIRS_PER_STEP, group_body, 0)


def _dsa_kernel(far_ref, x_ref, qt_ref, qit_ref, wt_ref, k_ref, ki_ref, vt_ref, tile_ref, wout_ref,
                o_ref, key_scr, hi_scr, lo_scr, madd_scr, logit_scr, ot_scr, *, top_k):
    i = pl.program_id(1)
    for n in range(1, k_ref.shape[2] // KC + 1):
        pl.when(i == n - 1)(functools.partial(
            _dsa_block, n, far_ref, qt_ref, qit_ref, wt_ref, k_ref, ki_ref, vt_ref, tile_ref,
            key_scr, hi_scr, lo_scr, madd_scr, logit_scr, ot_scr, top_k=top_k))
    o_ref[0] = x_ref[0] + _mm_tn(ot_scr[...].astype(_MXU_DTYPE), wout_ref[...])


def _dsa(x, far, qt, qit, wt, k, ki, vt, tiles, wout, top_k):
    b, s, d = x.shape
    aw = N_HEADS * HEAD_DIM
    iw = N_IDX_HEADS * IDX_DIM
    return pl.pallas_call(
        functools.partial(_dsa_kernel, top_k=top_k),
        grid=(b, s // TQ),
        in_specs=[
            pl.BlockSpec(memory_space=pltpu.SMEM),
            pl.BlockSpec((1, TQ, d), lambda bi, i: (bi, i, 0)),
            pl.BlockSpec((1, aw, TQ), lambda bi, i: (bi, 0, i)),
            pl.BlockSpec((1, iw, TQ), lambda bi, i: (bi, 0, i)),
            pl.BlockSpec((1, N_IDX_HEADS, TQ), lambda bi, i: (bi, 0, i)),
            pl.BlockSpec((1, aw // LANES, s, LANES), lambda bi, i: (bi, 0, 0, 0)),
            pl.BlockSpec((1, s, IDX_DIM), lambda bi, i: (bi, 0, 0)),
            pl.BlockSpec((1, aw, s), lambda bi, i: (bi, 0, 0)),
            _full(tiles.shape), _full(wout.shape),
        ],
        out_specs=pl.BlockSpec((1, TQ, d), lambda bi, i: (bi, i, 0)),
        out_shape=jax.ShapeDtypeStruct((b, s, d), F32),
        scratch_shapes=[
            pltpu.VMEM((s, TQ), I32),
            pltpu.VMEM((s, TQ), I16),
            pltpu.VMEM((s, TQ), I16),
            pltpu.VMEM((s, TQ), F32),
            pltpu.VMEM((2 * PAIRS_PER_STEP, s, TQ), F32),
            pltpu.VMEM((aw, TQ), F32),
        ],
        compiler_params=_params(),
        name="dsa",
    )(far, x, qt, qit, wt, k, ki, vt, tiles, wout)


def kernel(x, ffn1_norm, ffn1_w_gate, ffn1_w_up, ffn1_w_down, mix_norm, ffn2_norm, ffn2_w_gate,
           ffn2_w_up, ffn2_w_down, ab_w_in, pool_w, pool_scale, conv_w, conv_b, conv_ln_g,
           conv_ln_b, ab_w_out, c_w_in, c_w_out, rel_bias, final_norm):
    b, s, d = x.shape
    depth = ffn1_norm.shape[0]
    md = _MXU_DTYPE
    aw = N_HEADS * HEAD_DIM
    iw = N_IDX_HEADS * IDX_DIM
    top_k = min(TOPK_MAX, s // 4)
    assert s % MIX_TOKENS == 0 and (b * s) % FFN_TOKENS == 0 and s % TQ == 0 and top_k <= KC

    row = lambda v: v.reshape(1, -1)

    if depth > 1:
        sk = jnp.arange(KC, dtype=I32)[:, None]
        tq = jnp.arange(TQ, dtype=I32)[None, :]
        bkt = jnp.stack([_t5_bucket(jnp.maximum(tq + dl * TQ - sk, 0)) for dl in (0, 1)])
        tiles = _bias_tiles(rel_bias, bkt)
        assert 2 * TQ - KC + 1 >= REL_MAX_DIST
        far = rel_bias[REL_BUCKETS - 1]

    ffn1 = [w_.astype(md) for w_ in (ffn1_w_gate, ffn1_w_up, ffn1_w_down)]
    ffn2 = [w_.astype(md) for w_ in (ffn2_w_gate, ffn2_w_up, ffn2_w_down)]
    for layer in range(depth):
        i = layer // 2
        last = layer == depth - 1
        x2 = _ffn(x.reshape(b * s, d), row(ffn1_norm[layer]), ffn1[0][layer], ffn1[1][layer], ffn1[2][layer],
                  row(final_norm), False)
        x = x2.reshape(b, s, d)
        if layer % 2 == 0:
            x = _ab_mixer(x, row(mix_norm[layer]), ab_w_in[i].astype(md), pool_w[i].astype(md),
                          row(pool_scale[i]),
                          jnp.broadcast_to(conv_w[i][:, None, :], (CONV_KERNEL, SUBLANES, conv_w.shape[-1])),
                          row(conv_b[i]), row(conv_ln_g[i]),
                          row(conv_ln_b[i]), ab_w_out[i].astype(md))
        else:
            w = c_w_in[i]
            wq, wk, wv = w[:, 0:aw], w[:, aw:2 * aw], w[:, 2 * aw:3 * aw]
            wqi = w[:, 3 * aw:3 * aw + iw]
            wki = w[:, 3 * aw + iw:3 * aw + iw + IDX_DIM]
            www = w[:, 3 * aw + iw + IDX_DIM:]
            wn = jnp.concatenate([wk, wki], axis=1).astype(md)
            wt = jnp.concatenate([wq, wv, wqi], axis=1).T.astype(md)
            k, ki, qt, vt, qit, wts = _c_inproj(x, row(mix_norm[layer]), wn, wt, www.T.astype(md))
            x = _dsa(x, far, qt, qit, wts, k, ki, vt, tiles, c_w_out[i].astype(md), top_k)
        x2 = _ffn(x.reshape(b * s, d), row(ffn2_norm[layer]), ffn2[0][layer], ffn2[1][layer], ffn2[2][layer],
                  row(final_norm), last)
        x = x2.reshape(b, s, d)
    return x
```

```python
import functools
import math

import jax
import jax.numpy as jnp
from jax import lax
from jax.experimental import pallas as pl
from jax.experimental.pallas import tpu as pltpu

F32 = jnp.float32
I32 = jnp.int32
I16 = jnp.int16
_MXU_DTYPE = jnp.bfloat16

NORM_EPS = 1e-6
FFN_RES = 0.5
POOL_WINDOWS = (2, 4, 8, 16)
POOL_GROUP = 128
CONV_KERNEL = 31
N_HEADS = 16
HEAD_DIM = 64
N_IDX_HEADS = 8
IDX_DIM = 64
TOPK_MAX = 256
REL_BUCKETS = 32
REL_MAX_DIST = 128

V7X_VMEM_BYTES = 64 * 1024 * 1024
VMEM_LIMIT = V7X_VMEM_BYTES - 6 * 1024 * 1024
SUBLANES = 8
LANES = 128

FFN_TOKENS = 2048
FFN_ROW_GROUPS = 16
MIX_TOKENS = 512
CIN_TOKENS = 1024
CIN_ROW_GROUPS = 4
CONV_ROWS = 32
HALO = 32
TQ = 256
KC = 256
INT_MIN = -2 ** 31
HALF16 = 2 ** 15
PACK16 = 16
LOG2E = 1.0 / math.log(2.0)
PAIRS_PER_STEP = 2
ONES_ROWS = 16


def _rms(x, g):
    return x * lax.rsqrt(jnp.mean(x * x, axis=-1, keepdims=True) + NORM_EPS) * g


def _mm(a, b):
    return jnp.dot(a, b, preferred_element_type=F32)


def _mm_nt(a, b):
    return lax.dot_general(a, b, (((1,), (1,)), ((), ())), preferred_element_type=F32)


def _mm_tn(a, b):
    return lax.dot_general(a, b, (((0,), (0,)), ((), ())), preferred_element_type=F32)


def _params():
    return pltpu.CompilerParams(vmem_limit_bytes=VMEM_LIMIT)


def _full(shape):
    n = len(shape)
    return pl.BlockSpec(shape, lambda *_: (0,) * n)


def _ffn_kernel(x_ref, g_ref, wg_ref, wu_ref, wd_ref, fg_ref, o_ref, *, final_norm):
    t = x_ref.shape[0]
    for r0 in range(0, t, t // FFN_ROW_GROUPS):
        rows = slice(r0, r0 + t // FFN_ROW_GROUPS)
        x = x_ref[rows, :]
        h = _rms(x, g_ref[...]).astype(_MXU_DTYPE)
        gate = _mm(h, wg_ref[...])
        up = _mm(h, wu_ref[...])
        act = (gate * jax.nn.sigmoid(gate) * up).astype(_MXU_DTYPE)
        y = x + FFN_RES * _mm(act, wd_ref[...])
        if final_norm:
            y = _rms(y, fg_ref[...])
        o_ref[rows, :] = y


def _ffn(x2, g, wg, wu, wd, fg, final_norm):
    n, d = x2.shape
    f = wg.shape[1]
    t = FFN_TOKENS
    return pl.pallas_call(
        functools.partial(_ffn_kernel, final_norm=final_norm),
        grid=(n // t,),
        in_specs=[
            pl.BlockSpec((t, d), lambda i: (i, 0)),
            _full((1, d)), _full((d, f)), _full((d, f)), _full((f, d)), _full((1, d)),
        ],
        out_specs=pl.BlockSpec((t, d), lambda i: (i, 0)),
        out_shape=jax.ShapeDtypeStruct((n, d), F32),
        compiler_params=_params(),
        name="ffn_final" if final_norm else "ffn",
    )(x2, g, wg, wu, wd, fg)


def _ab_kernel(x_ref, g_ref, win_ref, pw_ref, ps_ref, cw_ref, cb_ref, lg_ref, lb_ref, wout_ref,
               o_ref, ubuf, zbuf, zsh, ycat, conv_scr):
    t = x_ref.shape[1]
    pw_width = ps_ref.shape[1]
    cw_width = cb_ref.shape[1]
    j = pl.program_id(1)

    @pl.when(j == 0)
    def _():
        ubuf[0:HALO, :] = jnp.zeros((HALO, pw_width), F32)
        zbuf[0:HALO, :] = jnp.zeros((HALO, cw_width), F32)

    @pl.when(j > 0)
    def _():
        ubuf[0:HALO, :] = ubuf[t:t + HALO, :]
        zbuf[0:HALO, :] = zbuf[t:t + HALO, :]

    x = x_ref[0]
    h = _rms(x, g_ref[...]).astype(_MXU_DTYPE)
    proj = _mm(h, win_ref[...])
    u = proj[:, 0:pw_width]
    val = proj[:, pw_width:pw_width + cw_width]
    gate = proj[:, pw_width + cw_width:pw_width + 2 * cw_width]
    ubuf[HALO:HALO + t, :] = u
    zbuf[HALO:HALO + t, :] = val * jax.nn.sigmoid(gate)
    for r in range(1, SUBLANES):
        zsh[r - 1, SUBLANES:HALO + t, :] = zbuf[SUBLANES - r:HALO + t - r, :]

    pos = j * t + lax.broadcasted_iota(I32, (t, POOL_GROUP), 0)
    for g, w in enumerate(POOL_WINDOWS):
        lo, hi = g * POOL_GROUP, (g + 1) * POOL_GROUP
        ug = ubuf[HALO:HALO + t, lo:hi]
        acc = ug
        for back in range(1, w):
            acc = acc + ubuf[HALO - back:HALO - back + t, lo:hi]
        count = jnp.minimum(pos + 1, w).astype(F32)
        diff = (acc / count - ug).astype(_MXU_DTYPE)
        ycat[:, lo:hi] = (_mm(diff, pw_ref[g]) * ps_ref[:, lo:hi]).astype(ycat.dtype)

    def conv_block(rb, carry):
        base = rb * CONV_ROWS
        groups = CONV_ROWS // SUBLANES
        acc = jnp.broadcast_to(cb_ref[...].reshape(1, 1, cw_width), (groups, SUBLANES, cw_width))
        for k in range(CONV_KERNEL):
            a, r = divmod(CONV_KERNEL - 1 - k, SUBLANES)
            start = pl.multiple_of(base + (HALO - a * SUBLANES), SUBLANES)
            src = zbuf if r == 0 else zsh.at[r - 1]
            win = src[pl.ds(start, CONV_ROWS), :].reshape(groups, SUBLANES, cw_width)
            acc = acc + win * cw_ref[k][None]
        conv_scr[pl.ds(pl.multiple_of(base, CONV_ROWS), CONV_ROWS), :] = acc.reshape(CONV_ROWS, cw_width)
        return carry

    lax.fori_loop(0, t // CONV_ROWS, conv_block, 0)
    conv = conv_scr[...]
    mu = jnp.mean(conv, axis=-1, keepdims=True)
    cen = conv - mu
    var = jnp.mean(cen * cen, axis=-1, keepdims=True)
    zn = cen * lax.rsqrt(var + NORM_EPS) * lg_ref[...] + lb_ref[...]
    ycat[:, pw_width:pw_width + cw_width] = (zn * jax.nn.sigmoid(zn)).astype(ycat.dtype)

    o_ref[0] = x + _mm(ycat[...], wout_ref[...])


def _ab_mixer(x, g, win, pw, ps, cw, cb, lg, lb, wout):
    b, s, d = x.shape
    t = MIX_TOKENS
    pwid, cwid = ps.shape[1], cb.shape[1]
    return pl.pallas_call(
        _ab_kernel,
        grid=(b, s // t),
        in_specs=[
            pl.BlockSpec((1, t, d), lambda bi, j: (bi, j, 0)),
            _full((1, d)), _full(win.shape), _full(pw.shape), _full(ps.shape), _full(cw.shape),
            _full(cb.shape), _full(lg.shape), _full(lb.shape), _full(wout.shape),
        ],
        out_specs=pl.BlockSpec((1, t, d), lambda bi, j: (bi, j, 0)),
        out_shape=jax.ShapeDtypeStruct((b, s, d), F32),
        scratch_shapes=[
            pltpu.VMEM((HALO + t, pwid), F32),
            pltpu.VMEM((HALO + t, cwid), F32),
            pltpu.VMEM((SUBLANES - 1, HALO + t, cwid), F32),
            pltpu.VMEM((t, pwid + cwid), _MXU_DTYPE),
            pltpu.VMEM((t, cwid), F32),
        ],
        compiler_params=_params(),
        name="ab_mixer",
    )(x, g, win, pw, ps, cw, cb, lg, lb, wout)


def _cin_kernel(x_ref, g_ref, wn_ref, wt_ref, ww_ref, k_out, ki_out, qt_out, vt_out, qit_out, wt_out):
    t = x_ref.shape[1]
    aw = qt_out.shape[1]
    iw = qit_out.shape[1]
    for r0 in range(0, t, t // CIN_ROW_GROUPS):
        rows = slice(r0, r0 + t // CIN_ROW_GROUPS)
        h = _rms(x_ref[0, rows, :], g_ref[...]).astype(_MXU_DTYPE)
        nat = _mm(h, wn_ref[...])
        for p in range(aw // LANES):
            k_out[0, p, rows, :] = nat[:, p * LANES:(p + 1) * LANES].astype(k_out.dtype)
        ki_out[0, rows, :] = nat[:, aw:aw + IDX_DIM].astype(ki_out.dtype)
        tr = _mm_nt(wt_ref[...], h)
        qt_out[0, :, rows] = (tr[0:aw] * (HEAD_DIM ** -0.5 * LOG2E)).astype(qt_out.dtype)
        vt_out[0, :, rows] = tr[aw:2 * aw].astype(vt_out.dtype)
        qit_out[0, :, rows] = tr[2 * aw:2 * aw + iw].astype(qit_out.dtype)
        wt_out[0, :, rows] = _mm_nt(ww_ref[...], h)


def _c_inproj(x, g, wn, wt, ww):
    b, s, d = x.shape
    t = CIN_TOKENS
    aw = N_HEADS * HEAD_DIM
    iw = N_IDX_HEADS * IDX_DIM
    md = _MXU_DTYPE
    return pl.pallas_call(
        _cin_kernel,
        grid=(b, s // t),
        in_specs=[
            pl.BlockSpec((1, t, d), lambda bi, j: (bi, j, 0)),
            _full((1, d)), _full(wn.shape), _full(wt.shape), _full(ww.shape),
        ],
        out_specs=[
            pl.BlockSpec((1, aw // LANES, t, LANES), lambda bi, j: (bi, 0, j, 0)),
            pl.BlockSpec((1, t, IDX_DIM), lambda bi, j: (bi, j, 0)),
            pl.BlockSpec((1, aw, t), lambda bi, j: (bi, 0, j)),
            pl.BlockSpec((1, aw, t), lambda bi, j: (bi, 0, j)),
            pl.BlockSpec((1, iw, t), lambda bi, j: (bi, 0, j)),
            pl.BlockSpec((1, N_IDX_HEADS, t), lambda bi, j: (bi, 0, j)),
        ],
        out_shape=[
            jax.ShapeDtypeStruct((b, aw // LANES, s, LANES), md),
            jax.ShapeDtypeStruct((b, s, IDX_DIM), md),
            jax.ShapeDtypeStruct((b, aw, s), md),
            jax.ShapeDtypeStruct((b, aw, s), md),
            jax.ShapeDtypeStruct((b, iw, s), md),
            jax.ShapeDtypeStruct((b, N_IDX_HEADS, s), F32),
        ],
        compiler_params=_params(),
        name="c_inproj",
    )(x, g, wn, wt, ww)


def _bias_kernel(rel_ref, bkt_ref, o_ref):
    h = pl.program_id(1)
    bkt = bkt_ref[0]
    acc = jnp.zeros(bkt.shape, F32)
    for b in range(REL_BUCKETS):
        acc = jnp.where(bkt == b, rel_ref[b, h], acc)
    o_ref[0, 0] = acc * LOG2E


def _bias_tiles(rel_bias, bkt):
    nd = bkt.shape[0]
    return pl.pallas_call(
        _bias_kernel,
        grid=(nd, N_HEADS),
        in_specs=[
            pl.BlockSpec(memory_space=pltpu.SMEM),
            pl.BlockSpec((1, KC, TQ), lambda dl, h: (dl, 0, 0)),
        ],
        out_specs=pl.BlockSpec((1, 1, KC, TQ), lambda dl, h: (dl, h, 0, 0)),
        out_shape=jax.ShapeDtypeStruct((nd, N_HEADS, KC, TQ), F32),
        name="bias_tiles",
    )(rel_bias, bkt)


def _t5_bucket(dist):
    max_exact = REL_BUCKETS // 2
    d = jnp.maximum(dist, max_exact).astype(F32)
    large = max_exact + (jnp.log(d / max_exact) / math.log(REL_MAX_DIST / max_exact)
                         * (REL_BUCKETS - max_exact)).astype(I32)
    large = jnp.minimum(large, REL_BUCKETS - 1)
    return jnp.where(dist < max_exact, dist, large)


def _colsum8(v):
    return v.reshape(v.shape[0] // SUBLANES, SUBLANES, v.shape[1]).sum(axis=0)


def _colmax8(v):
    return v.reshape(v.shape[0] // SUBLANES, SUBLANES, v.shape[1]).max(axis=0)


def _dsa_block(n, far_ref, qt_ref, qit_ref, wt_ref, k_ref, ki_ref, vt_ref, tile_ref,
               key_scr, hi_scr, lo_scr, madd_scr, logit_scr, ot_scr, *, top_k):
    idx_scale = (N_IDX_HEADS * IDX_DIM) ** -0.5
    causal = lax.broadcasted_iota(I32, (KC, TQ), 0) <= lax.broadcasted_iota(I32, (KC, TQ), 1)
    rows = lambda c: slice(c * KC, (c + 1) * KC)

    for c in range(n):
        ki_c = ki_ref[0, rows(c), :]
        acc = jnp.zeros((KC, TQ), F32)
        for h in range(N_IDX_HEADS):
            r = _mm(ki_c, qit_ref[0, h * IDX_DIM:(h + 1) * IDX_DIM, :])
            acc = acc + jnp.maximum(r, 0.0) * wt_ref[0, h:h + 1, :]
        bits = lax.bitcast_convert_type(acc * idx_scale, I32)
        key = jnp.where(bits < 0, bits ^ 0x7FFFFFFF, bits)
        if c == n - 1:
            key = jnp.where(causal, key, INT_MIN)
        key_scr[rows(c), :] = key
        hi_scr[rows(c), :] = (key >> 16).astype(I16)

    if n * KC <= top_k:
        for c in range(n):
            madd_scr[rows(c), :] = jnp.where(key_scr[rows(c), :] != INT_MIN, 0.0, -jnp.inf)
    else:
        def count16(ref, pred, t16):
            parts = []
            for c in range(n):
                hit = pred(ref[rows(c), :], t16).astype(I16)
                parts += [hit[r * PACK16:(r + 1) * PACK16] for r in range(KC // PACK16)]
            while len(parts) > 1:
                parts = [a + b for a, b in zip(parts[0::2], parts[1::2])] + parts[len(parts) & ~1:]
            return parts[0].astype(I32).sum(axis=0, keepdims=True)

        def search16(ref, want):
            def bit_body(it, t_u):
                cand = t_u | lax.shift_left(jnp.int32(1), 15 - it)
                cnt = count16(ref, lambda v, t_: v >= t_, (cand - HALF16).astype(I16))
                return jnp.where(cnt >= want, cand, t_u)
            return lax.fori_loop(0, 16, bit_body, jnp.zeros((1, TQ), I32)) - HALF16

        t_hi = search16(hi_scr, top_k)
        above = count16(hi_scr, lambda v, t_: v > t_, t_hi.astype(I16))
        for c in range(n):
            k = key_scr[rows(c), :]
            lo = jnp.where((k >> 16) == t_hi, (k & 0xFFFF) - HALF16, -HALF16)
            lo_scr[rows(c), :] = lo.astype(I16)
        t_lo = search16(lo_scr, top_k - above)
        thr = lax.shift_left(t_hi, 16) | (t_lo + HALF16)

        def count(pred, t_):
            acc = jnp.zeros((SUBLANES, TQ), I32)
            for c in range(n):
                acc = acc + _colsum8(pred(key_scr[rows(c), :], t_).astype(I32))
            return acc.sum(axis=0, keepdims=True)

        need = (top_k - count(lambda k, t_: k > t_, thr)).astype(F32)
        tri = (lax.broadcasted_iota(I32, (KC, KC), 1)
               < lax.broadcasted_iota(I32, (KC, KC), 0)).astype(_MXU_DTYPE)
        seen = jnp.zeros((1, TQ), F32)
        for c in range(n):
            k = key_scr[rows(c), :]
            eq = (k == thr).astype(F32)
            before = _mm(tri, eq.astype(_MXU_DTYPE)) + seen
            sel = ((k > thr) | ((eq > 0.0) & (before < need))) & (k != INT_MIN)
            madd_scr[rows(c), :] = jnp.where(sel, 0.0, -jnp.inf)
            seen = seen + _colsum8(eq).sum(axis=0, keepdims=True)

    pair_row = lax.broadcasted_iota(I32, (2 * HEAD_DIM, TQ), 0)
    ones_rows = jnp.ones((ONES_ROWS, KC), _MXU_DTYPE)
    n_far = max(n - 2, 0)

    def scores(p, slot):
        prow = pl.ds(pl.multiple_of(p * 2 * HEAD_DIM, 2 * HEAD_DIM), 2 * HEAD_DIM)
        qt = qt_ref[0, prow, :]
        zero = jnp.zeros_like(qt)
        q2 = jnp.concatenate([jnp.where(pair_row < HEAD_DIM, qt, zero),
                              jnp.where(pair_row >= HEAD_DIM, qt, zero)], axis=1)
        fars = [far_ref[2 * p + half] * LOG2E for half in range(2)]
        m_far = [jnp.full((SUBLANES, TQ), -jnp.inf, F32)] * 2
        m_near = list(m_far)
        for c in range(n):
            s2 = _mm(k_ref[0, p, rows(c), :], q2)
            for half in range(2):
                s = s2[:, half * TQ:(half + 1) * TQ] + madd_scr[rows(c), :]
                if c < n_far:
                    m_far[half] = jnp.maximum(m_far[half], _colmax8(s))
                else:
                    s = s + tile_ref[n - 1 - c, 2 * p + half]
                    m_near[half] = jnp.maximum(m_near[half], _colmax8(s))
                logit_scr[2 * slot + half, rows(c), :] = s
        return [jnp.maximum(m_far[half] + fars[half], m_near[half]).max(axis=0, keepdims=True)
                for half in range(2)]

    def weighted_values(p, slot, m):
        prow = pl.ds(pl.multiple_of(p * 2 * HEAD_DIM, 2 * HEAD_DIM), 2 * HEAD_DIM)
        fars = [far_ref[2 * p + half] * LOG2E for half in range(2)]
        acc = jnp.zeros((2 * HEAD_DIM + ONES_ROWS, 2 * TQ), F32)
        for c in range(n):
            e2 = jnp.concatenate(
                [jnp.exp2(logit_scr[2 * slot + half, rows(c), :]
                          - (m[half] - fars[half] if c < n_far else m[half])).astype(_MXU_DTYPE)
                 for half in range(2)], axis=1)
            lhs = jnp.concatenate([vt_ref[0, prow, rows(c)], ones_rows], axis=0)
            acc = acc + _mm(lhs, e2)
        for half in range(2):
            cols = slice(half * TQ, (half + 1) * TQ)
            ot_scr[pl.ds(pl.multiple_of((2 * p + half) * HEAD_DIM, HEAD_DIM), HEAD_DIM), :] = (
                acc[half * HEAD_DIM:(half + 1) * HEAD_DIM, cols]
                / acc[2 * HEAD_DIM:2 * HEAD_DIM + 1, cols])

    def group_body(g, carry):
        ms = [scores(g * PAIRS_PER_STEP + u, u) for u in range(PAIRS_PER_STEP)]
        for u in range(PAIRS_PER_STEP):
            weighted_values(g * PAIRS_PER_STEP + u, u, ms[u])
        return carry

    lax.fori_loop(0, N_HEADS // 2 // PAIRS_PER_STEP, group_body, 0)


def _dsa_kernel(far_ref, x_ref, qt_ref, qit_ref, wt_ref, k_ref, ki_ref, vt_ref, tile_ref, wout_ref,
                o_ref, key_scr, hi_scr, lo_scr, madd_scr, logit_scr, ot_scr, *, top_k):
    i = pl.program_id(1)
    for n in range(1, k_ref.shape[2] // KC + 1):
        pl.when(i == n - 1)(functools.partial(
            _dsa_block, n, far_ref, qt_ref, qit_ref, wt_ref, k_ref, ki_ref, vt_ref, tile_ref,
            key_scr, hi_scr, lo_scr, madd_scr, logit_scr, ot_scr, top_k=top_k))
    o_ref[0] = x_ref[0] + _mm_tn(ot_scr[...].astype(_MXU_DTYPE), wout_ref[...])


def _dsa(x, far, qt, qit, wt, k, ki, vt, tiles, wout, top_k):
    b, s, d = x.shape
    aw = N_HEADS * HEAD_DIM
    iw = N_IDX_HEADS * IDX_DIM
    return pl.pallas_call(
        functools.partial(_dsa_kernel, top_k=top_k),
        grid=(b, s // TQ),
        in_specs=[
            pl.BlockSpec(memory_space=pltpu.SMEM),
            pl.BlockSpec((1, TQ, d), lambda bi, i: (bi, i, 0)),
            pl.BlockSpec((1, aw, TQ), lambda bi, i: (bi, 0, i)),
            pl.BlockSpec((1, iw, TQ), lambda bi, i: (bi, 0, i)),
            pl.BlockSpec((1, N_IDX_HEADS, TQ), lambda bi, i: (bi, 0, i)),
            pl.BlockSpec((1, aw // LANES, s, LANES), lambda bi, i: (bi, 0, 0, 0)),
            pl.BlockSpec((1, s, IDX_DIM), lambda bi, i: (bi, 0, 0)),
            pl.BlockSpec((1, aw, s), lambda bi, i: (bi, 0, 0)),
            _full(tiles.shape), _full(wout.shape),
        ],
        out_specs=pl.BlockSpec((1, TQ, d), lambda bi, i: (bi, i, 0)),
        out_shape=jax.ShapeDtypeStruct((b, s, d), F32),
        scratch_shapes=[
            pltpu.VMEM((s, TQ), I32),
            pltpu.VMEM((s, TQ), I16),
            pltpu.VMEM((s, TQ), I16),
            pltpu.VMEM((s, TQ), F32),
            pltpu.VMEM((2 * PAIRS_PER_STEP, s, TQ), F32),
            pltpu.VMEM((aw, TQ), F32),
        ],
        compiler_params=_params(),
        name="dsa",
    )(far, x, qt, qit, wt, k, ki, vt, tiles, wout)


def kernel(x, ffn1_norm, ffn1_w_gate, ffn1_w_up, ffn1_w_down, mix_norm, ffn2_norm, ffn2_w_gate,
           ffn2_w_up, ffn2_w_down, ab_w_in, pool_w, pool_scale, conv_w, conv_b, conv_ln_g,
           conv_ln_b, ab_w_out, c_w_in, c_w_out, rel_bias, final_norm):
    b, s, d = x.shape
    depth = ffn1_norm.shape[0]
    md = _MXU_DTYPE
    aw = N_HEADS * HEAD_DIM
    iw = N_IDX_HEADS * IDX_DIM
    top_k = min(TOPK_MAX, s // 4)
    assert s % MIX_TOKENS == 0 and s % CIN_TOKENS == 0 and (b * s) % FFN_TOKENS == 0 and s % TQ == 0 and top_k <= KC

    row = lambda v: v.reshape(1, -1)

    if depth > 1:
        sk = jnp.arange(KC, dtype=I32)[:, None]
        tq = jnp.arange(TQ, dtype=I32)[None, :]
        bkt = jnp.stack([_t5_bucket(jnp.maximum(tq + dl * TQ - sk, 0)) for dl in (0, 1)])
        tiles = _bias_tiles(rel_bias, bkt)
        assert 2 * TQ - KC + 1 >= REL_MAX_DIST
        far = rel_bias[REL_BUCKETS - 1]

    ffn1 = [w_.astype(md) for w_ in (ffn1_w_gate, ffn1_w_up, ffn1_w_down)]
    ffn2 = [w_.astype(md) for w_ in (ffn2_w_gate, ffn2_w_up, ffn2_w_down)]
    for layer in range(depth):
        i = layer // 2
        last = layer == depth - 1
        x2 = _ffn(x.reshape(b * s, d), row(ffn1_norm[layer]), ffn1[0][layer], ffn1[1][layer], ffn1[2][layer],
                  row(final_norm), False)
        x = x2.reshape(b, s, d)
        if layer % 2 == 0:
            x = _ab_mixer(x, row(mix_norm[layer]), ab_w_in[i].astype(md), pool_w[i].astype(md),
                          row(pool_scale[i]),
                          jnp.broadcast_to(conv_w[i][:, None, :], (CONV_KERNEL, SUBLANES, conv_w.shape[-1])),
                          row(conv_b[i]), row(conv_ln_g[i]),
                          row(conv_ln_b[i]), ab_w_out[i].astype(md))
        else:
            w = c_w_in[i]
            wq, wk, wv = w[:, 0:aw], w[:, aw:2 * aw], w[:, 2 * aw:3 * aw]
            wqi = w[:, 3 * aw:3 * aw + iw]
            wki = w[:, 3 * aw + iw:3 * aw + iw + IDX_DIM]
            www = w[:, 3 * aw + iw + IDX_DIM:]
            wn = jnp.concatenate([wk, wki], axis=1).astype(md)
            wt = jnp.concatenate([wq, wv, wqi], axis=1).T.astype(md)
            k, ki, qt, vt, qit, wts = _c_inproj(x, row(mix_norm[layer]), wn, wt, www.T.astype(md))
            x = _dsa(x, far, qt, qit, wts, k, ki, vt, tiles, c_w_out[i].astype(md), top_k)
        x2 = _ffn(x.reshape(b * s, d), row(ffn2_norm[layer]), ffn2[0][layer], ffn2[1][layer], ffn2[2][layer],
                  row(final_norm), last)
        x = x2.reshape(b, s, d)
    return x
```

```python
import functools
import math

import jax
import jax.numpy as jnp
from jax import lax
from jax.experimental import pallas as pl
from jax.experimental.pallas import tpu as pltpu

F32 = jnp.float32
I32 = jnp.int32
I16 = jnp.int16
_MXU_DTYPE = jnp.bfloat16

NORM_EPS = 1e-6
FFN_RES = 0.5
POOL_WINDOWS = (2, 4, 8, 16)
POOL_GROUP = 128
CONV_KERNEL = 31
N_HEADS = 16
HEAD_DIM = 64
N_IDX_HEADS = 8
IDX_DIM = 64
TOPK_MAX = 256
REL_BUCKETS = 32
REL_MAX_DIST = 128

V7X_VMEM_BYTES = 64 * 1024 * 1024
VMEM_LIMIT = V7X_VMEM_BYTES - 6 * 1024 * 1024
SUBLANES = 8
LANES = 128

FFN_TOKENS = 512
FFN_ROW_GROUPS = 4
MIX_TOKENS = 512
CIN_TOKENS = 1024
CIN_ROW_GROUPS = 4
CONV_ROWS = 32
HALO = 32
TQ = 256
KC = 256
INT_MIN = -2 ** 31
HALF16 = 2 ** 15
PACK16 = 16
LOG2E = 1.0 / math.log(2.0)
PAIRS_PER_STEP = 2
ONES_ROWS = 16


def _rms(x, g):
    return x * lax.rsqrt(jnp.mean(x * x, axis=-1, keepdims=True) + NORM_EPS) * g


def _mm(a, b):
    return jnp.dot(a, b, preferred_element_type=F32)


def _mm_nt(a, b):
    return lax.dot_general(a, b, (((1,), (1,)), ((), ())), preferred_element_type=F32)


def _mm_tn(a, b):
    return lax.dot_general(a, b, (((0,), (0,)), ((), ())), preferred_element_type=F32)


def _params():
    return pltpu.CompilerParams(vmem_limit_bytes=VMEM_LIMIT)


def _full(shape):
    n = len(shape)
    return pl.BlockSpec(shape, lambda *_: (0,) * n)


def _ffn_kernel(x_ref, g_ref, wg_ref, wu_ref, wd_ref, fg_ref, o_ref, *, final_norm):
    t = x_ref.shape[0]
    for r0 in range(0, t, t // FFN_ROW_GROUPS):
        rows = slice(r0, r0 + t // FFN_ROW_GROUPS)
        x = x_ref[rows, :]
        h = _rms(x, g_ref[...]).astype(_MXU_DTYPE)
        gate = _mm(h, wg_ref[...])
        up = _mm(h, wu_ref[...])
        act = (gate * jax.nn.sigmoid(gate) * up).astype(_MXU_DTYPE)
        y = x + FFN_RES * _mm(act, wd_ref[...])
        if final_norm:
            y = _rms(y, fg_ref[...])
        o_ref[rows, :] = y


def _ffn(x2, g, wg, wu, wd, fg, final_norm):
    n, d = x2.shape
    f = wg.shape[1]
    t = FFN_TOKENS
    return pl.pallas_call(
        functools.partial(_ffn_kernel, final_norm=final_norm),
        grid=(n // t,),
        in_specs=[
            pl.BlockSpec((t, d), lambda i: (i, 0)),
            _full((1, d)), _full((d, f)), _full((d, f)), _full((f, d)), _full((1, d)),
        ],
        out_specs=pl.BlockSpec((t, d), lambda i: (i, 0)),
        out_shape=jax.ShapeDtypeStruct((n, d), F32),
        compiler_params=_params(),
        name="ffn_final" if final_norm else "ffn",
    )(x2, g, wg, wu, wd, fg)


def _ab_kernel(x_ref, g_ref, win_ref, pw_ref, ps_ref, cw_ref, cb_ref, lg_ref, lb_ref, wout_ref,
               o_ref, ubuf, zbuf, zsh, ycat, conv_scr):
    t = x_ref.shape[1]
    pw_width = ps_ref.shape[1]
    cw_width = cb_ref.shape[1]
    j = pl.program_id(1)

    @pl.when(j == 0)
    def _():
        ubuf[0:HALO, :] = jnp.zeros((HALO, pw_width), F32)
        zbuf[0:HALO, :] = jnp.zeros((HALO, cw_width), F32)

    @pl.when(j > 0)
    def _():
        ubuf[0:HALO, :] = ubuf[t:t + HALO, :]
        zbuf[0:HALO, :] = zbuf[t:t + HALO, :]

    x = x_ref[0]
    h = _rms(x, g_ref[...]).astype(_MXU_DTYPE)
    proj = _mm(h, win_ref[...])
    u = proj[:, 0:pw_width]
    val = proj[:, pw_width:pw_width + cw_width]
    gate = proj[:, pw_width + cw_width:pw_width + 2 * cw_width]
    ubuf[HALO:HALO + t, :] = u
    zbuf[HALO:HALO + t, :] = val * jax.nn.sigmoid(gate)
    for r in range(1, SUBLANES):
        zsh[r - 1, SUBLANES:HALO + t, :] = zbuf[SUBLANES - r:HALO + t - r, :]

    pos = j * t + lax.broadcasted_iota(I32, (t, POOL_GROUP), 0)
    for g, w in enumerate(POOL_WINDOWS):
        lo, hi = g * POOL_GROUP, (g + 1) * POOL_GROUP
        ug = ubuf[HALO:HALO + t, lo:hi]
        acc = ug
        for back in range(1, w):
            acc = acc + ubuf[HALO - back:HALO - back + t, lo:hi]
        count = jnp.minimum(pos + 1, w).astype(F32)
        diff = (acc / count - ug).astype(_MXU_DTYPE)
        ycat[:, lo:hi] = (_mm(diff, pw_ref[g]) * ps_ref[:, lo:hi]).astype(ycat.dtype)

    def conv_block(rb, carry):
        base = rb * CONV_ROWS
        groups = CONV_ROWS // SUBLANES
        acc = jnp.broadcast_to(cb_ref[...].reshape(1, 1, cw_width), (groups, SUBLANES, cw_width))
        for k in range(CONV_KERNEL):
            a, r = divmod(CONV_KERNEL - 1 - k, SUBLANES)
            start = pl.multiple_of(base + (HALO - a * SUBLANES), SUBLANES)
            src = zbuf if r == 0 else zsh.at[r - 1]
            win = src[pl.ds(start, CONV_ROWS), :].reshape(groups, SUBLANES, cw_width)
            acc = acc + win * cw_ref[k][None]
        conv_scr[pl.ds(pl.multiple_of(base, CONV_ROWS), CONV_ROWS), :] = acc.reshape(CONV_ROWS, cw_width)
        return carry

    lax.fori_loop(0, t // CONV_ROWS, conv_block, 0)
    conv = conv_scr[...]
    mu = jnp.mean(conv, axis=-1, keepdims=True)
    cen = conv - mu
    var = jnp.mean(cen * cen, axis=-1, keepdims=True)
    zn = cen * lax.rsqrt(var + NORM_EPS) * lg_ref[...] + lb_ref[...]
    ycat[:, pw_width:pw_width + cw_width] = (zn * jax.nn.sigmoid(zn)).astype(ycat.dtype)

    o_ref[0] = x + _mm(ycat[...], wout_ref[...])


def _ab_mixer(x, g, win, pw, ps, cw, cb, lg, lb, wout):
    b, s, d = x.shape
    t = MIX_TOKENS
    pwid, cwid = ps.shape[1], cb.shape[1]
    return pl.pallas_call(
        _ab_kernel,
        grid=(b, s // t),
        in_specs=[
            pl.BlockSpec((1, t, d), lambda bi, j: (bi, j, 0)),
            _full((1, d)), _full(win.shape), _full(pw.shape), _full(ps.shape), _full(cw.shape),
            _full(cb.shape), _full(lg.shape), _full(lb.shape), _full(wout.shape),
        ],
        out_specs=pl.BlockSpec((1, t, d), lambda bi, j: (bi, j, 0)),
        out_shape=jax.ShapeDtypeStruct((b, s, d), F32),
        scratch_shapes=[
            pltpu.VMEM((HALO + t, pwid), F32),
            pltpu.VMEM((HALO + t, cwid), F32),
            pltpu.VMEM((SUBLANES - 1, HALO + t, cwid), F32),
            pltpu.VMEM((t, pwid + cwid), _MXU_DTYPE),
            pltpu.VMEM((t, cwid), F32),
        ],
        compiler_params=_params(),
        name="ab_mixer",
    )(x, g, win, pw, ps, cw, cb, lg, lb, wout)


def _cin_kernel(x_ref, g_ref, wn_ref, wt_ref, ww_ref, k_out, ki_out, qt_out, vt_out, qit_out, wt_out):
    t = x_ref.shape[1]
    aw = qt_out.shape[1]
    iw = qit_out.shape[1]
    for r0 in range(0, t, t // CIN_ROW_GROUPS):
        rows = slice(r0, r0 + t // CIN_ROW_GROUPS)
        h = _rms(x_ref[0, rows, :], g_ref[...]).astype(_MXU_DTYPE)
        nat = _mm(h, wn_ref[...])
        for p in range(aw // LANES):
            k_out[0, p, rows, :] = nat[:, p * LANES:(p + 1) * LANES].astype(k_out.dtype)
        ki_out[0, rows, :] = nat[:, aw:aw + IDX_DIM].astype(ki_out.dtype)
        tr = _mm_nt(wt_ref[...], h)
        qt_out[0, :, rows] = (tr[0:aw] * (HEAD_DIM ** -0.5 * LOG2E)).astype(qt_out.dtype)
        vt_out[0, :, rows] = tr[aw:2 * aw].astype(vt_out.dtype)
        qit_out[0, :, rows] = tr[2 * aw:2 * aw + iw].astype(qit_out.dtype)
        wt_out[0, :, rows] = _mm_nt(ww_ref[...], h)


def _c_inproj(x, g, wn, wt, ww):
    b, s, d = x.shape
    t = CIN_TOKENS
    aw = N_HEADS * HEAD_DIM
    iw = N_IDX_HEADS * IDX_DIM
    md = _MXU_DTYPE
    return pl.pallas_call(
        _cin_kernel,
        grid=(b, s // t),
        in_specs=[
            pl.BlockSpec((1, t, d), lambda bi, j: (bi, j, 0)),
            _full((1, d)), _full(wn.shape), _full(wt.shape), _full(ww.shape),
        ],
        out_specs=[
            pl.BlockSpec((1, aw // LANES, t, LANES), lambda bi, j: (bi, 0, j, 0)),
            pl.BlockSpec((1, t, IDX_DIM), lambda bi, j: (bi, j, 0)),
            pl.BlockSpec((1, aw, t), lambda bi, j: (bi, 0, j)),
            pl.BlockSpec((1, aw, t), lambda bi, j: (bi, 0, j)),
            pl.BlockSpec((1, iw, t), lambda bi, j: (bi, 0, j)),
            pl.BlockSpec((1, N_IDX_HEADS, t), lambda bi, j: (bi, 0, j)),
        ],
        out_shape=[
            jax.ShapeDtypeStruct((b, aw // LANES, s, LANES), md),
            jax.ShapeDtypeStruct((b, s, IDX_DIM), md),
            jax.ShapeDtypeStruct((b, aw, s), md),
            jax.ShapeDtypeStruct((b, aw, s), md),
            jax.ShapeDtypeStruct((b, iw, s), md),
            jax.ShapeDtypeStruct((b, N_IDX_HEADS, s), F32),
        ],
        compiler_params=_params(),
        name="c_inproj",
    )(x, g, wn, wt, ww)


def _bias_kernel(rel_ref, bkt_ref, o_ref):
    h = pl.program_id(1)
    bkt = bkt_ref[0]
    acc = jnp.zeros(bkt.shape, F32)
    for b in range(REL_BUCKETS):
        acc = jnp.where(bkt == b, rel_ref[b, h], acc)
    o_ref[0, 0] = acc * LOG2E


def _bias_tiles(rel_bias, bkt):
    nd = bkt.shape[0]
    return pl.pallas_call(
        _bias_kernel,
        grid=(nd, N_HEADS),
        in_specs=[
            pl.BlockSpec(memory_space=pltpu.SMEM),
            pl.BlockSpec((1, KC, TQ), lambda dl, h: (dl, 0, 0)),
        ],
        out_specs=pl.BlockSpec((1, 1, KC, TQ), lambda dl, h: (dl, h, 0, 0)),
        out_shape=jax.ShapeDtypeStruct((nd, N_HEADS, KC, TQ), F32),
        name="bias_tiles",
    )(rel_bias, bkt)


def _t5_bucket(dist):
    max_exact = REL_BUCKETS // 2
    d = jnp.maximum(dist, max_exact).astype(F32)
    large = max_exact + (jnp.log(d / max_exact) / math.log(REL_MAX_DIST / max_exact)
                         * (REL_BUCKETS - max_exact)).astype(I32)
    large = jnp.minimum(large, REL_BUCKETS - 1)
    return jnp.where(dist < max_exact, dist, large)


def _colsum8(v):
    return v.reshape(v.shape[0] // SUBLANES, SUBLANES, v.shape[1]).sum(axis=0)


def _colmax8(v):
    return v.reshape(v.shape[0] // SUBLANES, SUBLANES, v.shape[1]).max(axis=0)


def _dsa_block(n, far_ref, qt_ref, qit_ref, wt_ref, k_ref, ki_ref, vt_ref, tile_ref,
               key_scr, hi_scr, lo_scr, madd_scr, logit_scr, ot_scr, *, top_k):
    idx_scale = (N_IDX_HEADS * IDX_DIM) ** -0.5
    causal = lax.broadcasted_iota(I32, (KC, TQ), 0) <= lax.broadcasted_iota(I32, (KC, TQ), 1)
    rows = lambda c: slice(c * KC, (c + 1) * KC)

    for c in range(n):
        ki_c = ki_ref[0, rows(c), :]
        acc = jnp.zeros((KC, TQ), F32)
        for h in range(N_IDX_HEADS):
            r = _mm(ki_c, qit_ref[0, h * IDX_DIM:(h + 1) * IDX_DIM, :])
            acc = acc + jnp.maximum(r, 0.0) * wt_ref[0, h:h + 1, :]
        bits = lax.bitcast_convert_type(acc * idx_scale, I32)
        key = jnp.where(bits < 0, bits ^ 0x7FFFFFFF, bits)
        if c == n - 1:
            key = jnp.where(causal, key, INT_MIN)
        key_scr[rows(c), :] = key
        hi_scr[rows(c), :] = (key >> 16).astype(I16)

    if n * KC <= top_k:
        for c in range(n):
            madd_scr[rows(c), :] = jnp.where(key_scr[rows(c), :] != INT_MIN, 0.0, -jnp.inf)
    else:
        def count16(ref, pred, t16):
            parts = []
            for c in range(n):
                hit = pred(ref[rows(c), :], t16).astype(I16)
                parts += [hit[r * PACK16:(r + 1) * PACK16] for r in range(KC // PACK16)]
            while len(parts) > 1:
                parts = [a + b for a, b in zip(parts[0::2], parts[1::2])] + parts[len(parts) & ~1:]
            return parts[0].astype(I32).sum(axis=0, keepdims=True)

        def search16(ref, want):
            def bit_body(it, t_u):
                cand = t_u | lax.shift_left(jnp.int32(1), 15 - it)
                cnt = count16(ref, lambda v, t_: v >= t_, (cand - HALF16).astype(I16))
                return jnp.where(cnt >= want, cand, t_u)
            return lax.fori_loop(0, 16, bit_body, jnp.zeros((1, TQ), I32)) - HALF16

        t_hi = search16(hi_scr, top_k)
        above = count16(hi_scr, lambda v, t_: v > t_, t_hi.astype(I16))
        for c in range(n):
            k = key_scr[rows(c), :]
            lo = jnp.where((k >> 16) == t_hi, (k & 0xFFFF) - HALF16, -HALF16)
            lo_scr[rows(c), :] = lo.astype(I16)
        t_lo = search16(lo_scr, top_k - above)
        thr = lax.shift_left(t_hi, 16) | (t_lo + HALF16)

        def count(pred, t_):
            acc = jnp.zeros((SUBLANES, TQ), I32)
            for c in range(n):
                acc = acc + _colsum8(pred(key_scr[rows(c), :], t_).astype(I32))
            return acc.sum(axis=0, keepdims=True)

        need = (top_k - count(lambda k, t_: k > t_, thr)).astype(F32)
        tri = (lax.broadcasted_iota(I32, (KC, KC), 1)
               < lax.broadcasted_iota(I32, (KC, KC), 0)).astype(_MXU_DTYPE)
        seen = jnp.zeros((1, TQ), F32)
        for c in range(n):
            k = key_scr[rows(c), :]
            eq = (k == thr).astype(F32)
            before = _mm(tri, eq.astype(_MXU_DTYPE)) + seen
            sel = ((k > thr) | ((eq > 0.0) & (before < need))) & (k != INT_MIN)
            madd_scr[rows(c), :] = jnp.where(sel, 0.0, -jnp.inf)
            seen = seen + _colsum8(eq).sum(axis=0, keepdims=True)

    pair_row = lax.broadcasted_iota(I32, (2 * HEAD_DIM, TQ), 0)
    ones_rows = jnp.ones((ONES_ROWS, KC), _MXU_DTYPE)
    n_far = max(n - 2, 0)

    def scores(p, slot):
        prow = pl.ds(pl.multiple_of(p * 2 * HEAD_DIM, 2 * HEAD_DIM), 2 * HEAD_DIM)
        qt = qt_ref[0, prow, :]
        zero = jnp.zeros_like(qt)
        q2 = jnp.concatenate([jnp.where(pair_row < HEAD_DIM, qt, zero),
                              jnp.where(pair_row >= HEAD_DIM, qt, zero)], axis=1)
        fars = [far_ref[2 * p + half] * LOG2E for half in range(2)]
        m_far = [jnp.full((SUBLANES, TQ), -jnp.inf, F32)] * 2
        m_near = list(m_far)
        for c in range(n):
            s2 = _mm(k_ref[0, p, rows(c), :], q2)
            for half in range(2):
                s = s2[:, half * TQ:(half + 1) * TQ] + madd_scr[rows(c), :]
                if c < n_far:
                    m_far[half] = jnp.maximum(m_far[half], _colmax8(s))
                else:
                    s = s + tile_ref[n - 1 - c, 2 * p + half]
                    m_near[half] = jnp.maximum(m_near[half], _colmax8(s))
                logit_scr[2 * slot + half, rows(c), :] = s
        return [jnp.maximum(m_far[half] + fars[half], m_near[half]).max(axis=0, keepdims=True)
                for half in range(2)]

    def weighted_values(p, slot, m):
        prow = pl.ds(pl.multiple_of(p * 2 * HEAD_DIM, 2 * HEAD_DIM), 2 * HEAD_DIM)
        fars = [far_ref[2 * p + half] * LOG2E for half in range(2)]
        acc = jnp.zeros((2 * HEAD_DIM + ONES_ROWS, 2 * TQ), F32)
        for c in range(n):
            e2 = jnp.concatenate(
                [jnp.exp2(logit_scr[2 * slot + half, rows(c), :]
                          - (m[half] - fars[half] if c < n_far else m[half])).astype(_MXU_DTYPE)
                 for half in range(2)], axis=1)
            lhs = jnp.concatenate([vt_ref[0, prow, rows(c)], ones_rows], axis=0)
            acc = acc + _mm(lhs, e2)
        for half in range(2):
            cols = slice(half * TQ, (half + 1) * TQ)
            ot_scr[pl.ds(pl.multiple_of((2 * p + half) * HEAD_DIM, HEAD_DIM), HEAD_DIM), :] = (
                acc[half * HEAD_DIM:(half + 1) * HEAD_DIM, cols]
                / acc[2 * HEAD_DIM:2 * HEAD_DIM + 1, cols])

    def group_body(g, carry):
        ms = [scores(g * PAIRS_PER_STEP + u, u) for u in range(PAIRS_PER_STEP)]
        for u in range(PAIRS_PER_STEP):
            weighted_values(g * PAIRS_PER_STEP + u, u, ms[u])
        return carry

    lax.fori_loop(0, N_HEADS // 2 // PAIRS_PER_STEP, group_body, 0)


def _dsa_kernel(far_ref, x_ref, qt_ref, qit_ref, wt_ref, k_ref, ki_ref, vt_ref, tile_ref, wout_ref,
                o_ref, key_scr, hi_scr, lo_scr, madd_scr, logit_scr, ot_scr, *, top_k):
    i = pl.program_id(1)
    for n in range(1, k_ref.shape[2] // KC + 1):
        pl.when(i == n - 1)(functools.partial(
            _dsa_block, n, far_ref, qt_ref, qit_ref, wt_ref, k_ref, ki_ref, vt_ref, tile_ref,
            key_scr, hi_scr, lo_scr, madd_scr, logit_scr, ot_scr, top_k=top_k))
    o_ref[0] = x_ref[0] + _mm_tn(ot_scr[...].astype(_MXU_DTYPE), wout_ref[...])


def _dsa(x, far, qt, qit, wt, k, ki, vt, tiles, wout, top_k):
    b, s, d = x.shape
    aw = N_HEADS * HEAD_DIM
    iw = N_IDX_HEADS * IDX_DIM
    return pl.pallas_call(
        functools.partial(_dsa_kernel, top_k=top_k),
        grid=(b, s // TQ),
        in_specs=[
            pl.BlockSpec(memory_space=pltpu.SMEM),
            pl.BlockSpec((1, TQ, d), lambda bi, i: (bi, i, 0)),
            pl.BlockSpec((1, aw, TQ), lambda bi, i: (bi, 0, i)),
            pl.BlockSpec((1, iw, TQ), lambda bi, i: (bi, 0, i)),
            pl.BlockSpec((1, N_IDX_HEADS, TQ), lambda bi, i: (bi, 0, i)),
            pl.BlockSpec((1, aw // LANES, s, LANES), lambda bi, i: (bi, 0, 0, 0)),
            pl.BlockSpec((1, s, IDX_DIM), lambda bi, i: (bi, 0, 0)),
            pl.BlockSpec((1, aw, s), lambda bi, i: (bi, 0, 0)),
            _full(tiles.shape), _full(wout.shape),
        ],
        out_specs=pl.BlockSpec((1, TQ, d), lambda bi, i: (bi, i, 0)),
        out_shape=jax.ShapeDtypeStruct((b, s, d), F32),
        scratch_shapes=[
            pltpu.VMEM((s, TQ), I32),
            pltpu.VMEM((s, TQ), I16),
            pltpu.VMEM((s, TQ), I16),
            pltpu.VMEM((s, TQ), F32),
            pltpu.VMEM((2 * PAIRS_PER_STEP, s, TQ), F32),
            pltpu.VMEM((aw, TQ), F32),
        ],
        compiler_params=_params(),
        name="dsa",
    )(far, x, qt, qit, wt, k, ki, vt, tiles, wout)


def kernel(x, ffn1_norm, ffn1_w_gate, ffn1_w_up, ffn1_w_down, mix_norm, ffn2_norm, ffn2_w_gate,
           ffn2_w_up, ffn2_w_down, ab_w_in, pool_w, pool_scale, conv_w, conv_b, conv_ln_g,
           conv_ln_b, ab_w_out, c_w_in, c_w_out, rel_bias, final_norm):
    b, s, d = x.shape
    depth = ffn1_norm.shape[0]
    md = _MXU_DTYPE
    aw = N_HEADS * HEAD_DIM
    iw = N_IDX_HEADS * IDX_DIM
    top_k = min(TOPK_MAX, s // 4)
    assert s % MIX_TOKENS == 0 and s % CIN_TOKENS == 0 and (b * s) % FFN_TOKENS == 0 and s % TQ == 0 and top_k <= KC

    row = lambda v: v.reshape(1, -1)

    if depth > 1:
        sk = jnp.arange(KC, dtype=I32)[:, None]
        tq = jnp.arange(TQ, dtype=I32)[None, :]
        bkt = jnp.stack([_t5_bucket(jnp.maximum(tq + dl * TQ - sk, 0)) for dl in (0, 1)])
        tiles = _bias_tiles(rel_bias, bkt)
        assert 2 * TQ - KC + 1 >= REL_MAX_DIST
        far = rel_bias[REL_BUCKETS - 1]

    ffn1 = [w_.astype(md) for w_ in (ffn1_w_gate, ffn1_w_up, ffn1_w_down)]
    ffn2 = [w_.astype(md) for w_ in (ffn2_w_gate, ffn2_w_up, ffn2_w_down)]
    for layer in range(depth):
        i = layer // 2
        last = layer == depth - 1
        x2 = _ffn(x.reshape(b * s, d), row(ffn1_norm[layer]), ffn1[0][layer], ffn1[1][layer], ffn1[2][layer],
                  row(final_norm), False)
        x = x2.reshape(b, s, d)
        if layer % 2 == 0:
            x = _ab_mixer(x, row(mix_norm[layer]), ab_w_in[i].astype(md), pool_w[i].astype(md),
                          row(pool_scale[i]),
                          jnp.broadcast_to(conv_w[i][:, None, :], (CONV_KERNEL, SUBLANES, conv_w.shape[-1])),
                          row(conv_b[i]), row(conv_ln_g[i]),
                          row(conv_ln_b[i]), ab_w_out[i].astype(md))
        else:
            w = c_w_in[i]
            wq, wk, wv = w[:, 0:aw], w[:, aw:2 * aw], w[:, 2 * aw:3 * aw]
            wqi = w[:, 3 * aw:3 * aw + iw]
            wki = w[:, 3 * aw + iw:3 * aw + iw + IDX_DIM]
            www = w[:, 3 * aw + iw + IDX_DIM:]
            wn = jnp.concatenate([wk, wki], axis=1).astype(md)
            wt = jnp.concatenate([wq, wv, wqi], axis=1).T.astype(md)
            k, ki, qt, vt, qit, wts = _c_inproj(x, row(mix_norm[layer]), wn, wt, www.T.astype(md))
            x = _dsa(x, far, qt, qit, wts, k, ki, vt, tiles, c_w_out[i].astype(md), top_k)
        x2 = _ffn(x.reshape(b * s, d), row(ffn2_norm[layer]), ffn2[0][layer], ffn2[1][layer], ffn2[2][layer],
                  row(final_norm), last)
        x = x2.reshape(b, s, d)
    return x
```

```python
import functools
import math

import jax
import jax.numpy as jnp
from jax import lax
from jax.experimental import pallas as pl
from jax.experimental.pallas import tpu as pltpu

F32 = jnp.float32
I32 = jnp.int32
I16 = jnp.int16
_MXU_DTYPE = jnp.bfloat16

NORM_EPS = 1e-6
FFN_RES = 0.5
POOL_WINDOWS = (2, 4, 8, 16)
POOL_GROUP = 128
CONV_KERNEL = 31
N_HEADS = 16
HEAD_DIM = 64
N_IDX_HEADS = 8
IDX_DIM = 64
TOPK_MAX = 256
REL_BUCKETS = 32
REL_MAX_DIST = 128

V7X_VMEM_BYTES = 64 * 1024 * 1024
VMEM_LIMIT = V7X_VMEM_BYTES - 6 * 1024 * 1024
SUBLANES = 8
LANES = 128

FFN_TOKENS = 2048
FFN_GROUP_ROWS = 128
FFN_ROW_GROUPS = 4
MIX_TOKENS = 512
CIN_TOKENS = 1024
CIN_ROW_GROUPS = 4
CONV_ROWS = 32
HALO = 32
TQ = 256
KC = 256
INT_MIN = -2 ** 31
HALF16 = 2 ** 15
PACK16 = 16
LOG2E = 1.0 / math.log(2.0)
PAIRS_PER_STEP = 2
ONES_ROWS = 16


def _rms(x, g):
    return x * lax.rsqrt(jnp.mean(x * x, axis=-1, keepdims=True) + NORM_EPS) * g


def _mm(a, b):
    return jnp.dot(a, b, preferred_element_type=F32)


def _mm_nt(a, b):
    return lax.dot_general(a, b, (((1,), (1,)), ((), ())), preferred_element_type=F32)


def _mm_tn(a, b):
    return lax.dot_general(a, b, (((0,), (0,)), ((), ())), preferred_element_type=F32)


def _params():
    return pltpu.CompilerParams(vmem_limit_bytes=VMEM_LIMIT)


def _full(shape):
    n = len(shape)
    return pl.BlockSpec(shape, lambda *_: (0,) * n)


def _ffn_kernel(x_ref, g_ref, wg_ref, wu_ref, wd_ref, fg_ref, o_ref, *, final_norm):
    chunk_rows = FFN_ROW_GROUPS * FFN_GROUP_ROWS

    def chunk_body(ci, carry):
        base = ci * chunk_rows
        for r in range(FFN_ROW_GROUPS):
            rows = pl.ds(pl.multiple_of(base + r * FFN_GROUP_ROWS, FFN_GROUP_ROWS), FFN_GROUP_ROWS)
            x = x_ref[rows, :]
            h = _rms(x, g_ref[...]).astype(_MXU_DTYPE)
            gate = _mm(h, wg_ref[...])
            up = _mm(h, wu_ref[...])
            act = (gate * jax.nn.sigmoid(gate) * up).astype(_MXU_DTYPE)
            y = x + FFN_RES * _mm(act, wd_ref[...])
            if final_norm:
                y = _rms(y, fg_ref[...])
            o_ref[rows, :] = y
        return carry

    lax.fori_loop(0, x_ref.shape[0] // chunk_rows, chunk_body, 0)


def _ffn(x2, g, wg, wu, wd, fg, final_norm):
    n, d = x2.shape
    f = wg.shape[1]
    t = FFN_TOKENS
    return pl.pallas_call(
        functools.partial(_ffn_kernel, final_norm=final_norm),
        grid=(n // t,),
        in_specs=[
            pl.BlockSpec((t, d), lambda i: (i, 0)),
            _full((1, d)), _full((d, f)), _full((d, f)), _full((f, d)), _full((1, d)),
        ],
        out_specs=pl.BlockSpec((t, d), lambda i: (i, 0)),
        out_shape=jax.ShapeDtypeStruct((n, d), F32),
        compiler_params=_params(),
        name="ffn_final" if final_norm else "ffn",
    )(x2, g, wg, wu, wd, fg)


def _ab_kernel(x_ref, g_ref, win_ref, pw_ref, ps_ref, cw_ref, cb_ref, lg_ref, lb_ref, wout_ref,
               o_ref, ubuf, zbuf, zsh, ycat, conv_scr):
    t = x_ref.shape[1]
    pw_width = ps_ref.shape[1]
    cw_width = cb_ref.shape[1]
    j = pl.program_id(1)

    @pl.when(j == 0)
    def _():
        ubuf[0:HALO, :] = jnp.zeros((HALO, pw_width), F32)
        zbuf[0:HALO, :] = jnp.zeros((HALO, cw_width), F32)

    @pl.when(j > 0)
    def _():
        ubuf[0:HALO, :] = ubuf[t:t + HALO, :]
        zbuf[0:HALO, :] = zbuf[t:t + HALO, :]

    x = x_ref[0]
    h = _rms(x, g_ref[...]).astype(_MXU_DTYPE)
    proj = _mm(h, win_ref[...])
    u = proj[:, 0:pw_width]
    val = proj[:, pw_width:pw_width + cw_width]
    gate = proj[:, pw_width + cw_width:pw_width + 2 * cw_width]
    ubuf[HALO:HALO + t, :] = u
    zbuf[HALO:HALO + t, :] = val * jax.nn.sigmoid(gate)
    for r in range(1, SUBLANES):
        zsh[r - 1, SUBLANES:HALO + t, :] = zbuf[SUBLANES - r:HALO + t - r, :]

    pos = j * t + lax.broadcasted_iota(I32, (t, POOL_GROUP), 0)
    for g, w in enumerate(POOL_WINDOWS):
        lo, hi = g * POOL_GROUP, (g + 1) * POOL_GROUP
        ug = ubuf[HALO:HALO + t, lo:hi]
        acc = ug
        for back in range(1, w):
            acc = acc + ubuf[HALO - back:HALO - back + t, lo:hi]
        count = jnp.minimum(pos + 1, w).astype(F32)
        diff = (acc / count - ug).astype(_MXU_DTYPE)
        ycat[:, lo:hi] = (_mm(diff, pw_ref[g]) * ps_ref[:, lo:hi]).astype(ycat.dtype)

    def conv_block(rb, carry):
        base = rb * CONV_ROWS
        groups = CONV_ROWS // SUBLANES
        acc = jnp.broadcast_to(cb_ref[...].reshape(1, 1, cw_width), (groups, SUBLANES, cw_width))
        for k in range(CONV_KERNEL):
            a, r = divmod(CONV_KERNEL - 1 - k, SUBLANES)
            start = pl.multiple_of(base + (HALO - a * SUBLANES), SUBLANES)
            src = zbuf if r == 0 else zsh.at[r - 1]
            win = src[pl.ds(start, CONV_ROWS), :].reshape(groups, SUBLANES, cw_width)
            acc = acc + win * cw_ref[k][None]
        conv_scr[pl.ds(pl.multiple_of(base, CONV_ROWS), CONV_ROWS), :] = acc.reshape(CONV_ROWS, cw_width)
        return carry

    lax.fori_loop(0, t // CONV_ROWS, conv_block, 0)
    conv = conv_scr[...]
    mu = jnp.mean(conv, axis=-1, keepdims=True)
    cen = conv - mu
    var = jnp.mean(cen * cen, axis=-1, keepdims=True)
    zn = cen * lax.rsqrt(var + NORM_EPS) * lg_ref[...] + lb_ref[...]
    ycat[:, pw_width:pw_width + cw_width] = (zn * jax.nn.sigmoid(zn)).astype(ycat.dtype)

    o_ref[0] = x + _mm(ycat[...], wout_ref[...])


def _ab_mixer(x, g, win, pw, ps, cw, cb, lg, lb, wout):
    b, s, d = x.shape
    t = MIX_TOKENS
    pwid, cwid = ps.shape[1], cb.shape[1]
    return pl.pallas_call(
        _ab_kernel,
        grid=(b, s // t),
        in_specs=[
            pl.BlockSpec((1, t, d), lambda bi, j: (bi, j, 0)),
            _full((1, d)), _full(win.shape), _full(pw.shape), _full(ps.shape), _full(cw.shape),
            _full(cb.shape), _full(lg.shape), _full(lb.shape), _full(wout.shape),
        ],
        out_specs=pl.BlockSpec((1, t, d), lambda bi, j: (bi, j, 0)),
        out_shape=jax.ShapeDtypeStruct((b, s, d), F32),
        scratch_shapes=[
            pltpu.VMEM((HALO + t, pwid), F32),
            pltpu.VMEM((HALO + t, cwid), F32),
            pltpu.VMEM((SUBLANES - 1, HALO + t, cwid), F32),
            pltpu.VMEM((t, pwid + cwid), _MXU_DTYPE),
            pltpu.VMEM((t, cwid), F32),
        ],
        compiler_params=_params(),
        name="ab_mixer",
    )(x, g, win, pw, ps, cw, cb, lg, lb, wout)


def _cin_kernel(x_ref, g_ref, wn_ref, wt_ref, ww_ref, k_out, ki_out, qt_out, vt_out, qit_out, wt_out):
    t = x_ref.shape[1]
    aw = qt_out.shape[1]
    iw = qit_out.shape[1]
    for r0 in range(0, t, t // CIN_ROW_GROUPS):
        rows = slice(r0, r0 + t // CIN_ROW_GROUPS)
        h = _rms(x_ref[0, rows, :], g_ref[...]).astype(_MXU_DTYPE)
        nat = _mm(h, wn_ref[...])
        for p in range(aw // LANES):
            k_out[0, p, rows, :] = nat[:, p * LANES:(p + 1) * LANES].astype(k_out.dtype)
        ki_out[0, rows, :] = nat[:, aw:aw + IDX_DIM].astype(ki_out.dtype)
        tr = _mm_nt(wt_ref[...], h)
        qt_out[0, :, rows] = (tr[0:aw] * (HEAD_DIM ** -0.5 * LOG2E)).astype(qt_out.dtype)
        vt_out[0, :, rows] = tr[aw:2 * aw].astype(vt_out.dtype)
        qit_out[0, :, rows] = tr[2 * aw:2 * aw + iw].astype(qit_out.dtype)
        wt_out[0, :, rows] = _mm_nt(ww_ref[...], h)


def _c_inproj(x, g, wn, wt, ww):
    b, s, d = x.shape
    t = CIN_TOKENS
    aw = N_HEADS * HEAD_DIM
    iw = N_IDX_HEADS * IDX_DIM
    md = _MXU_DTYPE
    return pl.pallas_call(
        _cin_kernel,
        grid=(b, s // t),
        in_specs=[
            pl.BlockSpec((1, t, d), lambda bi, j: (bi, j, 0)),
            _full((1, d)), _full(wn.shape), _full(wt.shape), _full(ww.shape),
        ],
        out_specs=[
            pl.BlockSpec((1, aw // LANES, t, LANES), lambda bi, j: (bi, 0, j, 0)),
            pl.BlockSpec((1, t, IDX_DIM), lambda bi, j: (bi, j, 0)),
            pl.BlockSpec((1, aw, t), lambda bi, j: (bi, 0, j)),
            pl.BlockSpec((1, aw, t), lambda bi, j: (bi, 0, j)),
            pl.BlockSpec((1, iw, t), lambda bi, j: (bi, 0, j)),
            pl.BlockSpec((1, N_IDX_HEADS, t), lambda bi, j: (bi, 0, j)),
        ],
        out_shape=[
            jax.ShapeDtypeStruct((b, aw // LANES, s, LANES), md),
            jax.ShapeDtypeStruct((b, s, IDX_DIM), md),
            jax.ShapeDtypeStruct((b, aw, s), md),
            jax.ShapeDtypeStruct((b, aw, s), md),
            jax.ShapeDtypeStruct((b, iw, s), md),
            jax.ShapeDtypeStruct((b, N_IDX_HEADS, s), F32),
        ],
        compiler_params=_params(),
        name="c_inproj",
    )(x, g, wn, wt, ww)


def _bias_kernel(rel_ref, bkt_ref, o_ref):
    h = pl.program_id(1)
    bkt = bkt_ref[0]
    acc = jnp.zeros(bkt.shape, F32)
    for b in range(REL_BUCKETS):
        acc = jnp.where(bkt == b, rel_ref[b, h], acc)
    o_ref[0, 0] = acc * LOG2E


def _bias_tiles(rel_bias, bkt):
    nd = bkt.shape[0]
    return pl.pallas_call(
        _bias_kernel,
        grid=(nd, N_HEADS),
        in_specs=[
            pl.BlockSpec(memory_space=pltpu.SMEM),
            pl.BlockSpec((1, KC, TQ), lambda dl, h: (dl, 0, 0)),
        ],
        out_specs=pl.BlockSpec((1, 1, KC, TQ), lambda dl, h: (dl, h, 0, 0)),
        out_shape=jax.ShapeDtypeStruct((nd, N_HEADS, KC, TQ), F32),
        name="bias_tiles",
    )(rel_bias, bkt)


def _t5_bucket(dist):
    max_exact = REL_BUCKETS // 2
    d = jnp.maximum(dist, max_exact).astype(F32)
    large = max_exact + (jnp.log(d / max_exact) / math.log(REL_MAX_DIST / max_exact)
                         * (REL_BUCKETS - max_exact)).astype(I32)
    large = jnp.minimum(large, REL_BUCKETS - 1)
    return jnp.where(dist < max_exact, dist, large)


def _colsum8(v):
    return v.reshape(v.shape[0] // SUBLANES, SUBLANES, v.shape[1]).sum(axis=0)


def _colmax8(v):
    return v.reshape(v.shape[0] // SUBLANES, SUBLANES, v.shape[1]).max(axis=0)


def _dsa_block(n, far_ref, qt_ref, qit_ref, wt_ref, k_ref, ki_ref, vt_ref, tile_ref,
               key_scr, hi_scr, lo_scr, madd_scr, logit_scr, ot_scr, *, top_k):
    idx_scale = (N_IDX_HEADS * IDX_DIM) ** -0.5
    causal = lax.broadcasted_iota(I32, (KC, TQ), 0) <= lax.broadcasted_iota(I32, (KC, TQ), 1)
    rows = lambda c: slice(c * KC, (c + 1) * KC)

    for c in range(n):
        ki_c = ki_ref[0, rows(c), :]
        acc = jnp.zeros((KC, TQ), F32)
        for h in range(N_IDX_HEADS):
            r = _mm(ki_c, qit_ref[0, h * IDX_DIM:(h + 1) * IDX_DIM, :])
            acc = acc + jnp.maximum(r, 0.0) * wt_ref[0, h:h + 1, :]
        bits = lax.bitcast_convert_type(acc * idx_scale, I32)
        key = jnp.where(bits < 0, bits ^ 0x7FFFFFFF, bits)
        if c == n - 1:
            key = jnp.where(causal, key, INT_MIN)
        key_scr[rows(c), :] = key
        hi_scr[rows(c), :] = (key >> 16).astype(I16)

    if n * KC <= top_k:
        for c in range(n):
            madd_scr[rows(c), :] = jnp.where(key_scr[rows(c), :] != INT_MIN, 0.0, -jnp.inf)
    else:
        def count16(ref, pred, t16):
            parts = []
            for c in range(n):
                hit = pred(ref[rows(c), :], t16).astype(I16)
                parts += [hit[r * PACK16:(r + 1) * PACK16] for r in range(KC // PACK16)]
            while len(parts) > 1:
                parts = [a + b for a, b in zip(parts[0::2], parts[1::2])] + parts[len(parts) & ~1:]
            return parts[0].astype(I32).sum(axis=0, keepdims=True)

        def search16(ref, want):
            def bit_body(it, t_u):
                cand = t_u | lax.shift_left(jnp.int32(1), 15 - it)
                cnt = count16(ref, lambda v, t_: v >= t_, (cand - HALF16).astype(I16))
                return jnp.where(cnt >= want, cand, t_u)
            return lax.fori_loop(0, 16, bit_body, jnp.zeros((1, TQ), I32)) - HALF16

        t_hi = search16(hi_scr, top_k)
        above = count16(hi_scr, lambda v, t_: v > t_, t_hi.astype(I16))
        for c in range(n):
            k = key_scr[rows(c), :]
            lo = jnp.where((k >> 16) == t_hi, (k & 0xFFFF) - HALF16, -HALF16)
            lo_scr[rows(c), :] = lo.astype(I16)
        t_lo = search16(lo_scr, top_k - above)
        thr = lax.shift_left(t_hi, 16) | (t_lo + HALF16)

        def count(pred, t_):
            acc = jnp.zeros((SUBLANES, TQ), I32)
            for c in range(n):
                acc = acc + _colsum8(pred(key_scr[rows(c), :], t_).astype(I32))
            return acc.sum(axis=0, keepdims=True)

        need = (top_k - count(lambda k, t_: k > t_, thr)).astype(F32)
        tri = (lax.broadcasted_iota(I32, (KC, KC), 1)
               < lax.broadcasted_iota(I32, (KC, KC), 0)).astype(_MXU_DTYPE)
        seen = jnp.zeros((1, TQ), F32)
        for c in range(n):
            k = key_scr[rows(c), :]
            eq = (k == thr).astype(F32)
            before = _mm(tri, eq.astype(_MXU_DTYPE)) + seen
            sel = ((k > thr) | ((eq > 0.0) & (before < need))) & (k != INT_MIN)
            madd_scr[rows(c), :] = jnp.where(sel, 0.0, -jnp.inf)
            seen = seen + _colsum8(eq).sum(axis=0, keepdims=True)

    pair_row = lax.broadcasted_iota(I32, (2 * HEAD_DIM, TQ), 0)
    ones_rows = jnp.ones((ONES_ROWS, KC), _MXU_DTYPE)
    n_far = max(n - 2, 0)

    def scores(p, slot):
        prow = pl.ds(pl.multiple_of(p * 2 * HEAD_DIM, 2 * HEAD_DIM), 2 * HEAD_DIM)
        qt = qt_ref[0, prow, :]
        zero = jnp.zeros_like(qt)
        q2 = jnp.concatenate([jnp.where(pair_row < HEAD_DIM, qt, zero),
                              jnp.where(pair_row >= HEAD_DIM, qt, zero)], axis=1)
        fars = [far_ref[2 * p + half] * LOG2E for half in range(2)]
        m_far = [jnp.full((SUBLANES, TQ), -jnp.inf, F32)] * 2
        m_near = list(m_far)
        for c in range(n):
            s2 = _mm(k_ref[0, p, rows(c), :], q2)
            for half in range(2):
                s = s2[:, half * TQ:(half + 1) * TQ] + madd_scr[rows(c), :]
                if c < n_far:
                    m_far[half] = jnp.maximum(m_far[half], _colmax8(s))
                else:
                    s = s + tile_ref[n - 1 - c, 2 * p + half]
                    m_near[half] = jnp.maximum(m_near[half], _colmax8(s))
                logit_scr[2 * slot + half, rows(c), :] = s
        return [jnp.maximum(m_far[half] + fars[half], m_near[half]).max(axis=0, keepdims=True)
                for half in range(2)]

    def weighted_values(p, slot, m):
        prow = pl.ds(pl.multiple_of(p * 2 * HEAD_DIM, 2 * HEAD_DIM), 2 * HEAD_DIM)
        fars = [far_ref[2 * p + half] * LOG2E for half in range(2)]
        acc = jnp.zeros((2 * HEAD_DIM + ONES_ROWS, 2 * TQ), F32)
        for c in range(n):
            e2 = jnp.concatenate(
                [jnp.exp2(logit_scr[2 * slot + half, rows(c), :]
                          - (m[half] - fars[half] if c < n_far else m[half])).astype(_MXU_DTYPE)
                 for half in range(2)], axis=1)
            lhs = jnp.concatenate([vt_ref[0, prow, rows(c)], ones_rows], axis=0)
            acc = acc + _mm(lhs, e2)
        for half in range(2):
            cols = slice(half * TQ, (half + 1) * TQ)
            ot_scr[pl.ds(pl.multiple_of((2 * p + half) * HEAD_DIM, HEAD_DIM), HEAD_DIM), :] = (
                acc[half * HEAD_DIM:(half + 1) * HEAD_DIM, cols]
                / acc[2 * HEAD_DIM:2 * HEAD_DIM + 1, cols])

    def group_body(g, carry):
        ms = [scores(g * PAIRS_PER_STEP + u, u) for u in range(PAIRS_PER_STEP)]
        for u in range(PAIRS_PER_STEP):
            weighted_values(g * PAIRS_PER_STEP + u, u, ms[u])
        return carry

    lax.fori_loop(0, N_HEADS // 2 // PAIRS_PER_STEP, group_body, 0)


def _dsa_kernel(far_ref, x_ref, qt_ref, qit_ref, wt_ref, k_ref, ki_ref, vt_ref, tile_ref, wout_ref,
                o_ref, key_scr, hi_scr, lo_scr, madd_scr, logit_scr, ot_scr, *, top_k):
    i = pl.program_id(1)
    for n in range(1, k_ref.shape[2] // KC + 1):
        pl.when(i == n - 1)(functools.partial(
            _dsa_block, n, far_ref, qt_ref, qit_ref, wt_ref, k_ref, ki_ref, vt_ref, tile_ref,
            key_scr, hi_scr, lo_scr, madd_scr, logit_scr, ot_scr, top_k=top_k))
    o_ref[0] = x_ref[0] + _mm_tn(ot_scr[...].astype(_MXU_DTYPE), wout_ref[...])


def _dsa(x, far, qt, qit, wt, k, ki, vt, tiles, wout, top_k):
    b, s, d = x.shape
    aw = N_HEADS * HEAD_DIM
    iw = N_IDX_HEADS * IDX_DIM
    return pl.pallas_call(
        functools.partial(_dsa_kernel, top_k=top_k),
        grid=(b, s // TQ),
        in_specs=[
            pl.BlockSpec(memory_space=pltpu.SMEM),
            pl.BlockSpec((1, TQ, d), lambda bi, i: (bi, i, 0)),
            pl.BlockSpec((1, aw, TQ), lambda bi, i: (bi, 0, i)),
            pl.BlockSpec((1, iw, TQ), lambda bi, i: (bi, 0, i)),
            pl.BlockSpec((1, N_IDX_HEADS, TQ), lambda bi, i: (bi, 0, i)),
            pl.BlockSpec((1, aw // LANES, s, LANES), lambda bi, i: (bi, 0, 0, 0)),
            pl.BlockSpec((1, s, IDX_DIM), lambda bi, i: (bi, 0, 0)),
            pl.BlockSpec((1, aw, s), lambda bi, i: (bi, 0, 0)),
            _full(tiles.shape), _full(wout.shape),
        ],
        out_specs=pl.BlockSpec((1, TQ, d), lambda bi, i: (bi, i, 0)),
        out_shape=jax.ShapeDtypeStruct((b, s, d), F32),
        scratch_shapes=[
            pltpu.VMEM((s, TQ), I32),
            pltpu.VMEM((s, TQ), I16),
            pltpu.VMEM((s, TQ), I16),
            pltpu.VMEM((s, TQ), F32),
            pltpu.VMEM((2 * PAIRS_PER_STEP, s, TQ), F32),
            pltpu.VMEM((aw, TQ), F32),
        ],
        compiler_params=_params(),
        name="dsa",
    )(far, x, qt, qit, wt, k, ki, vt, tiles, wout)


def kernel(x, ffn1_norm, ffn1_w_gate, ffn1_w_up, ffn1_w_down, mix_norm, ffn2_norm, ffn2_w_gate,
           ffn2_w_up, ffn2_w_down, ab_w_in, pool_w, pool_scale, conv_w, conv_b, conv_ln_g,
           conv_ln_b, ab_w_out, c_w_in, c_w_out, rel_bias, final_norm):
    b, s, d = x.shape
    depth = ffn1_norm.shape[0]
    md = _MXU_DTYPE
    aw = N_HEADS * HEAD_DIM
    iw = N_IDX_HEADS * IDX_DIM
    top_k = min(TOPK_MAX, s // 4)
    assert s % MIX_TOKENS == 0 and s % CIN_TOKENS == 0 and (b * s) % FFN_TOKENS == 0 and s % TQ == 0 and top_k <= KC

    row = lambda v: v.reshape(1, -1)

    if depth > 1:
        sk = jnp.arange(KC, dtype=I32)[:, None]
        tq = jnp.arange(TQ, dtype=I32)[None, :]
        bkt = jnp.stack([_t5_bucket(jnp.maximum(tq + dl * TQ - sk, 0)) for dl in (0, 1)])
        tiles = _bias_tiles(rel_bias, bkt)
        assert 2 * TQ - KC + 1 >= REL_MAX_DIST
        far = rel_bias[REL_BUCKETS - 1]

    ffn1 = [w_.astype(md) for w_ in (ffn1_w_gate, ffn1_w_up, ffn1_w_down)]
    ffn2 = [w_.astype(md) for w_ in (ffn2_w_gate, ffn2_w_up, ffn2_w_down)]
    for layer in range(depth):
        i = layer // 2
        last = layer == depth - 1
        x2 = _ffn(x.reshape(b * s, d), row(ffn1_norm[layer]), ffn1[0][layer], ffn1[1][layer], ffn1[2][layer],
                  row(final_norm), False)
        x = x2.reshape(b, s, d)
        if layer % 2 == 0:
            x = _ab_mixer(x, row(mix_norm[layer]), ab_w_in[i].astype(md), pool_w[i].astype(md),
                          row(pool_scale[i]),
                          jnp.broadcast_to(conv_w[i][:, None, :], (CONV_KERNEL, SUBLANES, conv_w.shape[-1])),
                          row(conv_b[i]), row(conv_ln_g[i]),
                          row(conv_ln_b[i]), ab_w_out[i].astype(md))
        else:
            w = c_w_in[i]
            wq, wk, wv = w[:, 0:aw], w[:, aw:2 * aw], w[:, 2 * aw:3 * aw]
            wqi = w[:, 3 * aw:3 * aw + iw]
            wki = w[:, 3 * aw + iw:3 * aw + iw + IDX_DIM]
            www = w[:, 3 * aw + iw + IDX_DIM:]
            wn = jnp.concatenate([wk, wki], axis=1).astype(md)
            wt = jnp.concatenate([wq, wv, wqi], axis=1).T.astype(md)
            k, ki, qt, vt, qit, wts = _c_inproj(x, row(mix_norm[layer]), wn, wt, www.T.astype(md))
            x = _dsa(x, far, qt, qit, wts, k, ki, vt, tiles, c_w_out[i].astype(md), top_k)
        x2 = _ffn(x.reshape(b * s, d), row(ffn2_norm[layer]), ffn2[0][layer], ffn2[1][layer], ffn2[2][layer],
                  row(final_norm), last)
        x = x2.reshape(b, s, d)
    return x
```

```python
import functools
import math

import jax
import jax.numpy as jnp
from jax import lax
from jax.experimental import pallas as pl
from jax.experimental.pallas import tpu as pltpu

F32 = jnp.float32
I32 = jnp.int32
I16 = jnp.int16
_MXU_DTYPE = jnp.bfloat16

NORM_EPS = 1e-6
FFN_RES = 0.5
POOL_WINDOWS = (2, 4, 8, 16)
POOL_GROUP = 128
CONV_KERNEL = 31
N_HEADS = 16
HEAD_DIM = 64
N_IDX_HEADS = 8
IDX_DIM = 64
TOPK_MAX = 256
REL_BUCKETS = 32
REL_MAX_DIST = 128

V7X_VMEM_BYTES = 64 * 1024 * 1024
VMEM_LIMIT = V7X_VMEM_BYTES - 6 * 1024 * 1024
SUBLANES = 8
LANES = 128

FFN_TOKENS = 512
FFN_ROW_GROUPS = 4
MIX_TOKENS = 512
CIN_TOKENS = 1024
CIN_ROW_GROUPS = 4
CONV_ROWS = 32
CAST_BLOCK_BYTES = 6 * 1024 * 1024
HALO = 32
TQ = 256
KC = 256
INT_MIN = -2 ** 31
HALF16 = 2 ** 15
PACK16 = 16
LOG2E = 1.0 / math.log(2.0)
PAIRS_PER_STEP = 2
ONES_ROWS = 16


def _rms(x, g):
    return x * lax.rsqrt(jnp.mean(x * x, axis=-1, keepdims=True) + NORM_EPS) * g


def _mm(a, b):
    return jnp.dot(a, b, preferred_element_type=F32)


def _mm_nt(a, b):
    return lax.dot_general(a, b, (((1,), (1,)), ((), ())), preferred_element_type=F32)


def _mm_tn(a, b):
    return lax.dot_general(a, b, (((0,), (0,)), ((), ())), preferred_element_type=F32)


def _params():
    return pltpu.CompilerParams(vmem_limit_bytes=VMEM_LIMIT)


def _full(shape):
    n = len(shape)
    return pl.BlockSpec(shape, lambda *_: (0,) * n)


def _cast_kernel(w_ref, o_ref):
    o_ref[...] = w_ref[...].astype(o_ref.dtype)


def _to_mxu(w):
    rows, cols = w.shape[-2], w.shape[-1]
    rb = rows
    while rb * cols * w.dtype.itemsize > CAST_BLOCK_BYTES and rb % 2 == 0 and (rb // 2) % (2 * SUBLANES) == 0:
        rb //= 2
    return pl.pallas_call(
        _cast_kernel,
        grid=(w.shape[0], rows // rb),
        in_specs=[pl.BlockSpec((1, rb, cols), lambda l, r: (l, r, 0))],
        out_specs=pl.BlockSpec((1, rb, cols), lambda l, r: (l, r, 0)),
        out_shape=jax.ShapeDtypeStruct(w.shape, _MXU_DTYPE),
        compiler_params=_params(),
        name="cast",
    )(w)


def _ffn_kernel(x_ref, g_ref, wg_ref, wu_ref, wd_ref, fg_ref, o_ref, *, final_norm):
    t = x_ref.shape[0]
    for r0 in range(0, t, t // FFN_ROW_GROUPS):
        rows = slice(r0, r0 + t // FFN_ROW_GROUPS)
        x = x_ref[rows, :]
        h = _rms(x, g_ref[...]).astype(_MXU_DTYPE)
        gate = _mm(h, wg_ref[...])
        up = _mm(h, wu_ref[...])
        act = (gate * jax.nn.sigmoid(gate) * up).astype(_MXU_DTYPE)
        y = x + FFN_RES * _mm(act, wd_ref[...])
        if final_norm:
            y = _rms(y, fg_ref[...])
        o_ref[rows, :] = y


def _ffn(x2, g, wg, wu, wd, fg, final_norm):
    n, d = x2.shape
    f = wg.shape[1]
    t = FFN_TOKENS
    return pl.pallas_call(
        functools.partial(_ffn_kernel, final_norm=final_norm),
        grid=(n // t,),
        in_specs=[
            pl.BlockSpec((t, d), lambda i: (i, 0)),
            _full((1, d)), _full((d, f)), _full((d, f)), _full((f, d)), _full((1, d)),
        ],
        out_specs=pl.BlockSpec((t, d), lambda i: (i, 0)),
        out_shape=jax.ShapeDtypeStruct((n, d), F32),
        compiler_params=_params(),
        name="ffn_final" if final_norm else "ffn",
    )(x2, g, wg, wu, wd, fg)


def _ab_kernel(x_ref, g_ref, win_ref, pw_ref, ps_ref, cw_ref, cb_ref, lg_ref, lb_ref, wout_ref,
               o_ref, ubuf, zbuf, zsh, ycat, conv_scr):
    t = x_ref.shape[1]
    pw_width = ps_ref.shape[1]
    cw_width = cb_ref.shape[1]
    j = pl.program_id(1)

    @pl.when(j == 0)
    def _():
        ubuf[0:HALO, :] = jnp.zeros((HALO, pw_width), F32)
        zbuf[0:HALO, :] = jnp.zeros((HALO, cw_width), F32)

    @pl.when(j > 0)
    def _():
        ubuf[0:HALO, :] = ubuf[t:t + HALO, :]
        zbuf[0:HALO, :] = zbuf[t:t + HALO, :]

    x = x_ref[0]
    h = _rms(x, g_ref[...]).astype(_MXU_DTYPE)
    proj = _mm(h, win_ref[...])
    u = proj[:, 0:pw_width]
    val = proj[:, pw_width:pw_width + cw_width]
    gate = proj[:, pw_width + cw_width:pw_width + 2 * cw_width]
    ubuf[HALO:HALO + t, :] = u
    zbuf[HALO:HALO + t, :] = val * jax.nn.sigmoid(gate)
    for r in range(1, SUBLANES):
        zsh[r - 1, SUBLANES:HALO + t, :] = zbuf[SUBLANES - r:HALO + t - r, :]

    pos = j * t + lax.broadcasted_iota(I32, (t, POOL_GROUP), 0)
    for g, w in enumerate(POOL_WINDOWS):
        lo, hi = g * POOL_GROUP, (g + 1) * POOL_GROUP
        ug = ubuf[HALO:HALO + t, lo:hi]
        acc = ug
        for back in range(1, w):
            acc = acc + ubuf[HALO - back:HALO - back + t, lo:hi]
        count = jnp.minimum(pos + 1, w).astype(F32)
        diff = (acc / count - ug).astype(_MXU_DTYPE)
        ycat[:, lo:hi] = (_mm(diff, pw_ref[g]) * ps_ref[:, lo:hi]).astype(ycat.dtype)

    def conv_block(rb, carry):
        base = rb * CONV_ROWS
        groups = CONV_ROWS // SUBLANES
        acc = jnp.broadcast_to(cb_ref[...].reshape(1, 1, cw_width), (groups, SUBLANES, cw_width))
        for k in range(CONV_KERNEL):
            a, r = divmod(CONV_KERNEL - 1 - k, SUBLANES)
            start = pl.multiple_of(base + (HALO - a * SUBLANES), SUBLANES)
            src = zbuf if r == 0 else zsh.at[r - 1]
            win = src[pl.ds(start, CONV_ROWS), :].reshape(groups, SUBLANES, cw_width)
            acc = acc + win * cw_ref[k][None]
        conv_scr[pl.ds(pl.multiple_of(base, CONV_ROWS), CONV_ROWS), :] = acc.reshape(CONV_ROWS, cw_width)
        return carry

    lax.fori_loop(0, t // CONV_ROWS, conv_block, 0)
    conv = conv_scr[...]
    mu = jnp.mean(conv, axis=-1, keepdims=True)
    cen = conv - mu
    var = jnp.mean(cen * cen, axis=-1, keepdims=True)
    zn = cen * lax.rsqrt(var + NORM_EPS) * lg_ref[...] + lb_ref[...]
    ycat[:, pw_width:pw_width + cw_width] = (zn * jax.nn.sigmoid(zn)).astype(ycat.dtype)

    o_ref[0] = x + _mm(ycat[...], wout_ref[...])


def _ab_mixer(x, g, win, pw, ps, cw, cb, lg, lb, wout):
    b, s, d = x.shape
    t = MIX_TOKENS
    pwid, cwid = ps.shape[1], cb.shape[1]
    return pl.pallas_call(
        _ab_kernel,
        grid=(b, s // t),
        in_specs=[
            pl.BlockSpec((1, t, d), lambda bi, j: (bi, j, 0)),
            _full((1, d)), _full(win.shape), _full(pw.shape), _full(ps.shape), _full(cw.shape),
            _full(cb.shape), _full(lg.shape), _full(lb.shape), _full(wout.shape),
        ],
        out_specs=pl.BlockSpec((1, t, d), lambda bi, j: (bi, j, 0)),
        out_shape=jax.ShapeDtypeStruct((b, s, d), F32),
        scratch_shapes=[
            pltpu.VMEM((HALO + t, pwid), F32),
            pltpu.VMEM((HALO + t, cwid), F32),
            pltpu.VMEM((SUBLANES - 1, HALO + t, cwid), F32),
            pltpu.VMEM((t, pwid + cwid), _MXU_DTYPE),
            pltpu.VMEM((t, cwid), F32),
        ],
        compiler_params=_params(),
        name="ab_mixer",
    )(x, g, win, pw, ps, cw, cb, lg, lb, wout)


def _cin_kernel(x_ref, g_ref, wn_ref, wt_ref, ww_ref, k_out, ki_out, qt_out, vt_out, qit_out, wt_out):
    t = x_ref.shape[1]
    aw = qt_out.shape[1]
    iw = qit_out.shape[1]
    for r0 in range(0, t, t // CIN_ROW_GROUPS):
        rows = slice(r0, r0 + t // CIN_ROW_GROUPS)
        h = _rms(x_ref[0, rows, :], g_ref[...]).astype(_MXU_DTYPE)
        nat = _mm(h, wn_ref[...])
        for p in range(aw // LANES):
            k_out[0, p, rows, :] = nat[:, p * LANES:(p + 1) * LANES].astype(k_out.dtype)
        ki_out[0, rows, :] = nat[:, aw:aw + IDX_DIM].astype(ki_out.dtype)
        tr = _mm_nt(wt_ref[...], h)
        qt_out[0, :, rows] = (tr[0:aw] * (HEAD_DIM ** -0.5 * LOG2E)).astype(qt_out.dtype)
        vt_out[0, :, rows] = tr[aw:2 * aw].astype(vt_out.dtype)
        qit_out[0, :, rows] = tr[2 * aw:2 * aw + iw].astype(qit_out.dtype)
        wt_out[0, :, rows] = _mm_nt(ww_ref[...], h)


def _c_inproj(x, g, wn, wt, ww):
    b, s, d = x.shape
    t = CIN_TOKENS
    aw = N_HEADS * HEAD_DIM
    iw = N_IDX_HEADS * IDX_DIM
    md = _MXU_DTYPE
    return pl.pallas_call(
        _cin_kernel,
        grid=(b, s // t),
        in_specs=[
            pl.BlockSpec((1, t, d), lambda bi, j: (bi, j, 0)),
            _full((1, d)), _full(wn.shape), _full(wt.shape), _full(ww.shape),
        ],
        out_specs=[
            pl.BlockSpec((1, aw // LANES, t, LANES), lambda bi, j: (bi, 0, j, 0)),
            pl.BlockSpec((1, t, IDX_DIM), lambda bi, j: (bi, j, 0)),
            pl.BlockSpec((1, aw, t), lambda bi, j: (bi, 0, j)),
            pl.BlockSpec((1, aw, t), lambda bi, j: (bi, 0, j)),
            pl.BlockSpec((1, iw, t), lambda bi, j: (bi, 0, j)),
            pl.BlockSpec((1, N_IDX_HEADS, t), lambda bi, j: (bi, 0, j)),
        ],
        out_shape=[
            jax.ShapeDtypeStruct((b, aw // LANES, s, LANES), md),
            jax.ShapeDtypeStruct((b, s, IDX_DIM), md),
            jax.ShapeDtypeStruct((b, aw, s), md),
            jax.ShapeDtypeStruct((b, aw, s), md),
            jax.ShapeDtypeStruct((b, iw, s), md),
            jax.ShapeDtypeStruct((b, N_IDX_HEADS, s), F32),
        ],
        compiler_params=_params(),
        name="c_inproj",
    )(x, g, wn, wt, ww)


def _bias_kernel(rel_ref, bkt_ref, o_ref):
    h = pl.program_id(1)
    bkt = bkt_ref[0]
    acc = jnp.zeros(bkt.shape, F32)
    for b in range(REL_BUCKETS):
        acc = jnp.where(bkt == b, rel_ref[b, h], acc)
    o_ref[0, 0] = acc * LOG2E


def _bias_tiles(rel_bias, bkt):
    nd = bkt.shape[0]
    return pl.pallas_call(
        _bias_kernel,
        grid=(nd, N_HEADS),
        in_specs=[
            pl.BlockSpec(memory_space=pltpu.SMEM),
            pl.BlockSpec((1, KC, TQ), lambda dl, h: (dl, 0, 0)),
        ],
        out_specs=pl.BlockSpec((1, 1, KC, TQ), lambda dl, h: (dl, h, 0, 0)),
        out_shape=jax.ShapeDtypeStruct((nd, N_HEADS, KC, TQ), F32),
        name="bias_tiles",
    )(rel_bias, bkt)


def _t5_bucket(dist):
    max_exact = REL_BUCKETS // 2
    d = jnp.maximum(dist, max_exact).astype(F32)
    large = max_exact + (jnp.log(d / max_exact) / math.log(REL_MAX_DIST / max_exact)
                         * (REL_BUCKETS - max_exact)).astype(I32)
    large = jnp.minimum(large, REL_BUCKETS - 1)
    return jnp.where(dist < max_exact, dist, large)


def _colsum8(v):
    return v.reshape(v.shape[0] // SUBLANES, SUBLANES, v.shape[1]).sum(axis=0)


def _colmax8(v):
    return v.reshape(v.shape[0] // SUBLANES, SUBLANES, v.shape[1]).max(axis=0)


def _dsa_block(n, far_ref, qt_ref, qit_ref, wt_ref, k_ref, ki_ref, vt_ref, tile_ref,
               key_scr, hi_scr, lo_scr, madd_scr, logit_scr, ot_scr, *, top_k):
    idx_scale = (N_IDX_HEADS * IDX_DIM) ** -0.5
    causal = lax.broadcasted_iota(I32, (KC, TQ), 0) <= lax.broadcasted_iota(I32, (KC, TQ), 1)
    rows = lambda c: slice(c * KC, (c + 1) * KC)

    for c in range(n):
        ki_c = ki_ref[0, rows(c), :]
        acc = jnp.zeros((KC, TQ), F32)
        for h in range(N_IDX_HEADS):
            r = _mm(ki_c, qit_ref[0, h * IDX_DIM:(h + 1) * IDX_DIM, :])
            acc = acc + jnp.maximum(r, 0.0) * wt_ref[0, h:h + 1, :]
        bits = lax.bitcast_convert_type(acc * idx_scale, I32)
        key = jnp.where(bits < 0, bits ^ 0x7FFFFFFF, bits)
        if c == n - 1:
            key = jnp.where(causal, key, INT_MIN)
        key_scr[rows(c), :] = key
        hi_scr[rows(c), :] = (key >> 16).astype(I16)

    if n * KC <= top_k:
        for c in range(n):
            madd_scr[rows(c), :] = jnp.where(key_scr[rows(c), :] != INT_MIN, 0.0, -jnp.inf)
    else:
        def count16(ref, pred, t16):
            parts = []
            for c in range(n):
                hit = pred(ref[rows(c), :], t16).astype(I16)
                parts += [hit[r * PACK16:(r + 1) * PACK16] for r in range(KC // PACK16)]
            while len(parts) > 1:
                parts = [a + b for a, b in zip(parts[0::2], parts[1::2])] + parts[len(parts) & ~1:]
            return parts[0].astype(I32).sum(axis=0, keepdims=True)

        def search16(ref, want):
            def bit_body(it, t_u):
                cand = t_u | lax.shift_left(jnp.int32(1), 15 - it)
                cnt = count16(ref, lambda v, t_: v >= t_, (cand - HALF16).astype(I16))
                return jnp.where(cnt >= want, cand, t_u)
            return lax.fori_loop(0, 16, bit_body, jnp.zeros((1, TQ), I32)) - HALF16

        t_hi = search16(hi_scr, top_k)
        above = count16(hi_scr, lambda v, t_: v > t_, t_hi.astype(I16))
        for c in range(n):
            k = key_scr[rows(c), :]
            lo = jnp.where((k >> 16) == t_hi, (k & 0xFFFF) - HALF16, -HALF16)
            lo_scr[rows(c), :] = lo.astype(I16)
        t_lo = search16(lo_scr, top_k - above)
        thr = lax.shift_left(t_hi, 16) | (t_lo + HALF16)

        def count(pred, t_):
            acc = jnp.zeros((SUBLANES, TQ), I32)
            for c in range(n):
                acc = acc + _colsum8(pred(key_scr[rows(c), :], t_).astype(I32))
            return acc.sum(axis=0, keepdims=True)

        at_least = count(lambda k, t_: k >= t_, thr)
        exact = jnp.max(jnp.where(at_least == top_k, 0, 1)) == 0

        @pl.when(exact)
        def _():
            for c in range(n):
                k = key_scr[rows(c), :]
                madd_scr[rows(c), :] = jnp.where((k >= thr) & (k != INT_MIN), 0.0, -jnp.inf)

        @pl.when(jnp.logical_not(exact))
        def _():
            need = (top_k - count(lambda k, t_: k > t_, thr)).astype(F32)
            tri = (lax.broadcasted_iota(I32, (KC, KC), 1)
                   < lax.broadcasted_iota(I32, (KC, KC), 0)).astype(_MXU_DTYPE)
            seen = jnp.zeros((1, TQ), F32)
            for c in range(n):
                k = key_scr[rows(c), :]
                eq = (k == thr).astype(F32)
                before = _mm(tri, eq.astype(_MXU_DTYPE)) + seen
                sel = ((k > thr) | ((eq > 0.0) & (before < need))) & (k != INT_MIN)
                madd_scr[rows(c), :] = jnp.where(sel, 0.0, -jnp.inf)
                seen = seen + _colsum8(eq).sum(axis=0, keepdims=True)

    pair_row = lax.broadcasted_iota(I32, (2 * HEAD_DIM, TQ), 0)
    ones_rows = jnp.ones((ONES_ROWS, KC), _MXU_DTYPE)
    n_far = max(n - 2, 0)

    def scores(p, slot):
        prow = pl.ds(pl.multiple_of(p * 2 * HEAD_DIM, 2 * HEAD_DIM), 2 * HEAD_DIM)
        qt = qt_ref[0, prow, :]
        zero = jnp.zeros_like(qt)
        q2 = jnp.concatenate([jnp.where(pair_row < HEAD_DIM, qt, zero),
                              jnp.where(pair_row >= HEAD_DIM, qt, zero)], axis=1)
        fars = [far_ref[2 * p + half] * LOG2E for half in range(2)]
        m_far = [jnp.full((SUBLANES, TQ), -jnp.inf, F32)] * 2
        m_near = list(m_far)
        for c in range(n):
            s2 = _mm(k_ref[0, p, rows(c), :], q2)
            for half in range(2):
                s = s2[:, half * TQ:(half + 1) * TQ] + madd_scr[rows(c), :]
                if c < n_far:
                    m_far[half] = jnp.maximum(m_far[half], _colmax8(s))
                else:
                    s = s + tile_ref[n - 1 - c, 2 * p + half]
                    m_near[half] = jnp.maximum(m_near[half], _colmax8(s))
                logit_scr[2 * slot + half, rows(c), :] = s
        return [jnp.maximum(m_far[half] + fars[half], m_near[half]).max(axis=0, keepdims=True)
                for half in range(2)]

    def weighted_values(p, slot, m):
        prow = pl.ds(pl.multiple_of(p * 2 * HEAD_DIM, 2 * HEAD_DIM), 2 * HEAD_DIM)
        fars = [far_ref[2 * p + half] * LOG2E for half in range(2)]
        acc = jnp.zeros((2 * HEAD_DIM + ONES_ROWS, 2 * TQ), F32)
        for c in range(n):
            e2 = jnp.concatenate(
                [jnp.exp2(logit_scr[2 * slot + half, rows(c), :]
                          - (m[half] - fars[half] if c < n_far else m[half])).astype(_MXU_DTYPE)
                 for half in range(2)], axis=1)
            lhs = jnp.concatenate([vt_ref[0, prow, rows(c)], ones_rows], axis=0)
            acc = acc + _mm(lhs, e2)
        for half in range(2):
            cols = slice(half * TQ, (half + 1) * TQ)
            ot_scr[pl.ds(pl.multiple_of((2 * p + half) * HEAD_DIM, HEAD_DIM), HEAD_DIM), :] = (
                acc[half * HEAD_DIM:(half + 1) * HEAD_DIM, cols]
                / acc[2 * HEAD_DIM:2 * HEAD_DIM + 1, cols])

    def group_body(g, carry):
        ms = [scores(g * PAIRS_PER_STEP + u, u) for u in range(PAIRS_PER_STEP)]
        for u in range(PAIRS_PER_STEP):
            weighted_values(g * PAIRS_PER_STEP + u, u, ms[u])
        return carry

    lax.fori_loop(0, N_HEADS // 2 // PAIRS_PER_STEP, group_body, 0)


def _dsa_kernel(far_ref, x_ref, qt_ref, qit_ref, wt_ref, k_ref, ki_ref, vt_ref, tile_ref, wout_ref,
                o_ref, key_scr, hi_scr, lo_scr, madd_scr, logit_scr, ot_scr, *, top_k):
    i = pl.program_id(1)
    for n in range(1, k_ref.shape[2] // KC + 1):
        pl.when(i == n - 1)(functools.partial(
            _dsa_block, n, far_ref, qt_ref, qit_ref, wt_ref, k_ref, ki_ref, vt_ref, tile_ref,
            key_scr, hi_scr, lo_scr, madd_scr, logit_scr, ot_scr, top_k=top_k))
    o_ref[0] = x_ref[0] + _mm_tn(ot_scr[...].astype(_MXU_DTYPE), wout_ref[...])


def _dsa(x, far, qt, qit, wt, k, ki, vt, tiles, wout, top_k):
    b, s, d = x.shape
    aw = N_HEADS * HEAD_DIM
    iw = N_IDX_HEADS * IDX_DIM
    return pl.pallas_call(
        functools.partial(_dsa_kernel, top_k=top_k),
        grid=(b, s // TQ),
        in_specs=[
            pl.BlockSpec(memory_space=pltpu.SMEM),
            pl.BlockSpec((1, TQ, d), lambda bi, i: (bi, i, 0)),
            pl.BlockSpec((1, aw, TQ), lambda bi, i: (bi, 0, i)),
            pl.BlockSpec((1, iw, TQ), lambda bi, i: (bi, 0, i)),
            pl.BlockSpec((1, N_IDX_HEADS, TQ), lambda bi, i: (bi, 0, i)),
            pl.BlockSpec((1, aw // LANES, s, LANES), lambda bi, i: (bi, 0, 0, 0)),
            pl.BlockSpec((1, s, IDX_DIM), lambda bi, i: (bi, 0, 0)),
            pl.BlockSpec((1, aw, s), lambda bi, i: (bi, 0, 0)),
            _full(tiles.shape), _full(wout.shape),
        ],
        out_specs=pl.BlockSpec((1, TQ, d), lambda bi, i: (bi, i, 0)),
        out_shape=jax.ShapeDtypeStruct((b, s, d), F32),
        scratch_shapes=[
            pltpu.VMEM((s, TQ), I32),
            pltpu.VMEM((s, TQ), I16),
            pltpu.VMEM((s, TQ), I16),
            pltpu.VMEM((s, TQ), F32),
            pltpu.VMEM((2 * PAIRS_PER_STEP, s, TQ), F32),
            pltpu.VMEM((aw, TQ), F32),
        ],
        compiler_params=_params(),
        name="dsa",
    )(far, x, qt, qit, wt, k, ki, vt, tiles, wout)


def kernel(x, ffn1_norm, ffn1_w_gate, ffn1_w_up, ffn1_w_down, mix_norm, ffn2_norm, ffn2_w_gate,
           ffn2_w_up, ffn2_w_down, ab_w_in, pool_w, pool_scale, conv_w, conv_b, conv_ln_g,
           conv_ln_b, ab_w_out, c_w_in, c_w_out, rel_bias, final_norm):
    b, s, d = x.shape
    depth = ffn1_norm.shape[0]
    md = _MXU_DTYPE
    aw = N_HEADS * HEAD_DIM
    iw = N_IDX_HEADS * IDX_DIM
    top_k = min(TOPK_MAX, s // 4)
    assert s % MIX_TOKENS == 0 and s % CIN_TOKENS == 0 and (b * s) % FFN_TOKENS == 0 and s % TQ == 0 and top_k <= KC

    row = lambda v: v.reshape(1, -1)

    if depth > 1:
        sk = jnp.arange(KC, dtype=I32)[:, None]
        tq = jnp.arange(TQ, dtype=I32)[None, :]
        bkt = jnp.stack([_t5_bucket(jnp.maximum(tq + dl * TQ - sk, 0)) for dl in (0, 1)])
        tiles = _bias_tiles(rel_bias, bkt)
        assert 2 * TQ - KC + 1 >= REL_MAX_DIST
        far = rel_bias[REL_BUCKETS - 1]

    ffn1 = [_to_mxu(w_) for w_ in (ffn1_w_gate, ffn1_w_up, ffn1_w_down)]
    ffn2 = [_to_mxu(w_) for w_ in (ffn2_w_gate, ffn2_w_up, ffn2_w_down)]
    for layer in range(depth):
        i = layer // 2
        last = layer == depth - 1
        x2 = _ffn(x.reshape(b * s, d), row(ffn1_norm[layer]), ffn1[0][layer], ffn1[1][layer], ffn1[2][layer],
                  row(final_norm), False)
        x = x2.reshape(b, s, d)
        if layer % 2 == 0:
            x = _ab_mixer(x, row(mix_norm[layer]), ab_w_in[i].astype(md), pool_w[i].astype(md),
                          row(pool_scale[i]),
                          jnp.broadcast_to(conv_w[i][:, None, :], (CONV_KERNEL, SUBLANES, conv_w.shape[-1])),
                          row(conv_b[i]), row(conv_ln_g[i]),
                          row(conv_ln_b[i]), ab_w_out[i].astype(md))
        else:
            w = c_w_in[i]
            wq, wk, wv = w[:, 0:aw], w[:, aw:2 * aw], w[:, 2 * aw:3 * aw]
            wqi = w[:, 3 * aw:3 * aw + iw]
            wki = w[:, 3 * aw + iw:3 * aw + iw + IDX_DIM]
            www = w[:, 3 * aw + iw + IDX_DIM:]
            wn = jnp.concatenate([wk, wki], axis=1).astype(md)
            wt = jnp.concatenate([wq, wv, wqi], axis=1).T.astype(md)
            k, ki, qt, vt, qit, wts = _c_inproj(x, row(mix_norm[layer]), wn, wt, www.T.astype(md))
            x = _dsa(x, far, qt, qit, wts, k, ki, vt, tiles, c_w_out[i].astype(md), top_k)
        x2 = _ffn(x.reshape(b * s, d), row(ffn2_norm[layer]), ffn2[0][layer], ffn2[1][layer], ffn2[2][layer],
                  row(final_norm), last)
        x = x2.reshape(b, s, d)
    return x
```

```python
import functools
import math

import jax
import jax.numpy as jnp
from jax import lax
from jax.experimental import pallas as pl
from jax.experimental.pallas import tpu as pltpu

F32 = jnp.float32
I32 = jnp.int32
I16 = jnp.int16
_MXU_DTYPE = jnp.bfloat16

NORM_EPS = 1e-6
FFN_RES = 0.5
POOL_WINDOWS = (2, 4, 8, 16)
POOL_GROUP = 128
CONV_KERNEL = 31
N_HEADS = 16
HEAD_DIM = 64
N_IDX_HEADS = 8
IDX_DIM = 64
TOPK_MAX = 256
REL_BUCKETS = 32
REL_MAX_DIST = 128

V7X_VMEM_BYTES = 64 * 1024 * 1024
VMEM_LIMIT = V7X_VMEM_BYTES - 6 * 1024 * 1024
SUBLANES = 8
LANES = 128

FFN_TOKENS = 512
FFN_ROW_GROUPS = 4
MIX_TOKENS = 512
CIN_TOKENS = 1024
CIN_ROW_GROUPS = 4
CONV_ROWS = 32
CAST_BLOCK_BYTES = 6 * 1024 * 1024
HALO = 32
TQ = 256
KC = 256
INT_MIN = -2 ** 31
HALF16 = 2 ** 15
PACK16 = 16
LOG2E = 1.0 / math.log(2.0)
PAIRS_PER_STEP = 2
ONES_ROWS = 16


def _rms(x, g):
    return x * lax.rsqrt(jnp.mean(x * x, axis=-1, keepdims=True) + NORM_EPS) * g


def _mm(a, b):
    return jnp.dot(a, b, preferred_element_type=F32)


def _mm_nt(a, b):
    return lax.dot_general(a, b, (((1,), (1,)), ((), ())), preferred_element_type=F32)


def _mm_tn(a, b):
    return lax.dot_general(a, b, (((0,), (0,)), ((), ())), preferred_element_type=F32)


def _params():
    return pltpu.CompilerParams(vmem_limit_bytes=VMEM_LIMIT)


def _full(shape):
    n = len(shape)
    return pl.BlockSpec(shape, lambda *_: (0,) * n)


def _cast_kernel(w_ref, o_ref):
    o_ref[...] = w_ref[...].astype(o_ref.dtype)


def _to_mxu(w):
    rows, cols = w.shape[-2], w.shape[-1]
    rb = rows
    while rb * cols * w.dtype.itemsize > CAST_BLOCK_BYTES and rb % 2 == 0 and (rb // 2) % (2 * SUBLANES) == 0:
        rb //= 2
    return pl.pallas_call(
        _cast_kernel,
        grid=(w.shape[0], rows // rb),
        in_specs=[pl.BlockSpec((1, rb, cols), lambda l, r: (l, r, 0))],
        out_specs=pl.BlockSpec((1, rb, cols), lambda l, r: (l, r, 0)),
        out_shape=jax.ShapeDtypeStruct(w.shape, _MXU_DTYPE),
        compiler_params=_params(),
        name="cast",
    )(w)


def _ffn_kernel(x_ref, g_ref, wg_ref, wu_ref, wd_ref, fg_ref, o_ref, *, final_norm):
    t = x_ref.shape[0]
    for r0 in range(0, t, t // FFN_ROW_GROUPS):
        rows = slice(r0, r0 + t // FFN_ROW_GROUPS)
        x = x_ref[rows, :]
        h = _rms(x, g_ref[...]).astype(_MXU_DTYPE)
        gate = _mm(h, wg_ref[...])
        up = _mm(h, wu_ref[...])
        act = (gate * jax.nn.sigmoid(gate) * up).astype(_MXU_DTYPE)
        y = x + FFN_RES * _mm(act, wd_ref[...])
        if final_norm:
            y = _rms(y, fg_ref[...])
        o_ref[rows, :] = y


def _ffn(x2, g, wg, wu, wd, fg, final_norm, layer):
    n, d = x2.shape
    f = wg.shape[2]
    t = FFN_TOKENS
    of_layer = lambda rows, cols: pl.BlockSpec((None, rows, cols), lambda i: (layer, 0, 0))
    return pl.pallas_call(
        functools.partial(_ffn_kernel, final_norm=final_norm),
        grid=(n // t,),
        in_specs=[
            pl.BlockSpec((t, d), lambda i: (i, 0)),
            _full((1, d)), of_layer(d, f), of_layer(d, f), of_layer(f, d), _full((1, d)),
        ],
        out_specs=pl.BlockSpec((t, d), lambda i: (i, 0)),
        out_shape=jax.ShapeDtypeStruct((n, d), F32),
        compiler_params=_params(),
        name="ffn_final" if final_norm else "ffn",
    )(x2, g, wg, wu, wd, fg)


def _ab_kernel(x_ref, g_ref, win_ref, pw_ref, ps_ref, cw_ref, cb_ref, lg_ref, lb_ref, wout_ref,
               o_ref, ubuf, zbuf, zsh, ycat, conv_scr):
    t = x_ref.shape[1]
    pw_width = ps_ref.shape[1]
    cw_width = cb_ref.shape[1]
    j = pl.program_id(1)

    @pl.when(j == 0)
    def _():
        ubuf[0:HALO, :] = jnp.zeros((HALO, pw_width), F32)
        zbuf[0:HALO, :] = jnp.zeros((HALO, cw_width), F32)

    @pl.when(j > 0)
    def _():
        ubuf[0:HALO, :] = ubuf[t:t + HALO, :]
        zbuf[0:HALO, :] = zbuf[t:t + HALO, :]

    x = x_ref[0]
    h = _rms(x, g_ref[...]).astype(_MXU_DTYPE)
    proj = _mm(h, win_ref[...])
    u = proj[:, 0:pw_width]
    val = proj[:, pw_width:pw_width + cw_width]
    gate = proj[:, pw_width + cw_width:pw_width + 2 * cw_width]
    ubuf[HALO:HALO + t, :] = u
    zbuf[HALO:HALO + t, :] = val * jax.nn.sigmoid(gate)
    for r in range(1, SUBLANES):
        zsh[r - 1, SUBLANES:HALO + t, :] = zbuf[SUBLANES - r:HALO + t - r, :]

    pos = j * t + lax.broadcasted_iota(I32, (t, POOL_GROUP), 0)
    for g, w in enumerate(POOL_WINDOWS):
        lo, hi = g * POOL_GROUP, (g + 1) * POOL_GROUP
        ug = ubuf[HALO:HALO + t, lo:hi]
        acc = ug
        for back in range(1, w):
            acc = acc + ubuf[HALO - back:HALO - back + t, lo:hi]
        count = jnp.minimum(pos + 1, w).astype(F32)
        diff = (acc / count - ug).astype(_MXU_DTYPE)
        ycat[:, lo:hi] = (_mm(diff, pw_ref[g]) * ps_ref[:, lo:hi]).astype(ycat.dtype)

    def conv_block(rb, carry):
        base = rb * CONV_ROWS
        groups = CONV_ROWS // SUBLANES
        acc = jnp.broadcast_to(cb_ref[...].reshape(1, 1, cw_width), (groups, SUBLANES, cw_width))
        for k in range(CONV_KERNEL):
            a, r = divmod(CONV_KERNEL - 1 - k, SUBLANES)
            start = pl.multiple_of(base + (HALO - a * SUBLANES), SUBLANES)
            src = zbuf if r == 0 else zsh.at[r - 1]
            win = src[pl.ds(start, CONV_ROWS), :].reshape(groups, SUBLANES, cw_width)
            acc = acc + win * cw_ref[k][None]
        conv_scr[pl.ds(pl.multiple_of(base, CONV_ROWS), CONV_ROWS), :] = acc.reshape(CONV_ROWS, cw_width)
        return carry

    lax.fori_loop(0, t // CONV_ROWS, conv_block, 0)
    conv = conv_scr[...]
    mu = jnp.mean(conv, axis=-1, keepdims=True)
    cen = conv - mu
    var = jnp.mean(cen * cen, axis=-1, keepdims=True)
    zn = cen * lax.rsqrt(var + NORM_EPS) * lg_ref[...] + lb_ref[...]
    ycat[:, pw_width:pw_width + cw_width] = (zn * jax.nn.sigmoid(zn)).astype(ycat.dtype)

    o_ref[0] = x + _mm(ycat[...], wout_ref[...])


def _ab_mixer(x, g, win, pw, ps, cw, cb, lg, lb, wout):
    b, s, d = x.shape
    t = MIX_TOKENS
    pwid, cwid = ps.shape[1], cb.shape[1]
    return pl.pallas_call(
        _ab_kernel,
        grid=(b, s // t),
        in_specs=[
            pl.BlockSpec((1, t, d), lambda bi, j: (bi, j, 0)),
            _full((1, d)), _full(win.shape), _full(pw.shape), _full(ps.shape), _full(cw.shape),
            _full(cb.shape), _full(lg.shape), _full(lb.shape), _full(wout.shape),
        ],
        out_specs=pl.BlockSpec((1, t, d), lambda bi, j: (bi, j, 0)),
        out_shape=jax.ShapeDtypeStruct((b, s, d), F32),
        scratch_shapes=[
            pltpu.VMEM((HALO + t, pwid), F32),
            pltpu.VMEM((HALO + t, cwid), F32),
            pltpu.VMEM((SUBLANES - 1, HALO + t, cwid), F32),
            pltpu.VMEM((t, pwid + cwid), _MXU_DTYPE),
            pltpu.VMEM((t, cwid), F32),
        ],
        compiler_params=_params(),
        name="ab_mixer",
    )(x, g, win, pw, ps, cw, cb, lg, lb, wout)


def _cin_kernel(x_ref, g_ref, wn_ref, wt_ref, ww_ref, k_out, ki_out, qt_out, vt_out, qit_out, wt_out):
    t = x_ref.shape[1]
    aw = qt_out.shape[1]
    iw = qit_out.shape[1]
    for r0 in range(0, t, t // CIN_ROW_GROUPS):
        rows = slice(r0, r0 + t // CIN_ROW_GROUPS)
        h = _rms(x_ref[0, rows, :], g_ref[...]).astype(_MXU_DTYPE)
        nat = _mm(h, wn_ref[...])
        for p in range(aw // LANES):
            k_out[0, p, rows, :] = nat[:, p * LANES:(p + 1) * LANES].astype(k_out.dtype)
        ki_out[0, rows, :] = nat[:, aw:aw + IDX_DIM].astype(ki_out.dtype)
        tr = _mm_nt(wt_ref[...], h)
        qt_out[0, :, rows] = (tr[0:aw] * (HEAD_DIM ** -0.5 * LOG2E)).astype(qt_out.dtype)
        vt_out[0, :, rows] = tr[aw:2 * aw].astype(vt_out.dtype)
        qit_out[0, :, rows] = tr[2 * aw:2 * aw + iw].astype(qit_out.dtype)
        wt_out[0, :, rows] = _mm_nt(ww_ref[...], h)


def _c_inproj(x, g, wn, wt, ww):
    b, s, d = x.shape
    t = CIN_TOKENS
    aw = N_HEADS * HEAD_DIM
    iw = N_IDX_HEADS * IDX_DIM
    md = _MXU_DTYPE
    return pl.pallas_call(
        _cin_kernel,
        grid=(b, s // t),
        in_specs=[
            pl.BlockSpec((1, t, d), lambda bi, j: (bi, j, 0)),
            _full((1, d)), _full(wn.shape), _full(wt.shape), _full(ww.shape),
        ],
        out_specs=[
            pl.BlockSpec((1, aw // LANES, t, LANES), lambda bi, j: (bi, 0, j, 0)),
            pl.BlockSpec((1, t, IDX_DIM), lambda bi, j: (bi, j, 0)),
            pl.BlockSpec((1, aw, t), lambda bi, j: (bi, 0, j)),
            pl.BlockSpec((1, aw, t), lambda bi, j: (bi, 0, j)),
            pl.BlockSpec((1, iw, t), lambda bi, j: (bi, 0, j)),
            pl.BlockSpec((1, N_IDX_HEADS, t), lambda bi, j: (bi, 0, j)),
        ],
        out_shape=[
            jax.ShapeDtypeStruct((b, aw // LANES, s, LANES), md),
            jax.ShapeDtypeStruct((b, s, IDX_DIM), md),
            jax.ShapeDtypeStruct((b, aw, s), md),
            jax.ShapeDtypeStruct((b, aw, s), md),
            jax.ShapeDtypeStruct((b, iw, s), md),
            jax.ShapeDtypeStruct((b, N_IDX_HEADS, s), F32),
        ],
        compiler_params=_params(),
        name="c_inproj",
    )(x, g, wn, wt, ww)


def _bias_kernel(rel_ref, bkt_ref, o_ref):
    h = pl.program_id(1)
    bkt = bkt_ref[0]
    acc = jnp.zeros(bkt.shape, F32)
    for b in range(REL_BUCKETS):
        acc = jnp.where(bkt == b, rel_ref[b, h], acc)
    o_ref[0, 0] = acc * LOG2E


def _bias_tiles(rel_bias, bkt):
    nd = bkt.shape[0]
    return pl.pallas_call(
        _bias_kernel,
        grid=(nd, N_HEADS),
        in_specs=[
            pl.BlockSpec(memory_space=pltpu.SMEM),
            pl.BlockSpec((1, KC, TQ), lambda dl, h: (dl, 0, 0)),
        ],
        out_specs=pl.BlockSpec((1, 1, KC, TQ), lambda dl, h: (dl, h, 0, 0)),
        out_shape=jax.ShapeDtypeStruct((nd, N_HEADS, KC, TQ), F32),
        name="bias_tiles",
    )(rel_bias, bkt)


def _t5_bucket(dist):
    max_exact = REL_BUCKETS // 2
    d = jnp.maximum(dist, max_exact).astype(F32)
    large = max_exact + (jnp.log(d / max_exact) / math.log(REL_MAX_DIST / max_exact)
                         * (REL_BUCKETS - max_exact)).astype(I32)
    large = jnp.minimum(large, REL_BUCKETS - 1)
    return jnp.where(dist < max_exact, dist, large)


def _colsum8(v):
    return v.reshape(v.shape[0] // SUBLANES, SUBLANES, v.shape[1]).sum(axis=0)


def _colmax8(v):
    return v.reshape(v.shape[0] // SUBLANES, SUBLANES, v.shape[1]).max(axis=0)


def _dsa_block(n, far_ref, qt_ref, qit_ref, wt_ref, k_ref, ki_ref, vt_ref, tile_ref,
               key_scr, hi_scr, lo_scr, madd_scr, logit_scr, ot_scr, *, top_k):
    idx_scale = (N_IDX_HEADS * IDX_DIM) ** -0.5
    causal = lax.broadcasted_iota(I32, (KC, TQ), 0) <= lax.broadcasted_iota(I32, (KC, TQ), 1)
    rows = lambda c: slice(c * KC, (c + 1) * KC)

    for c in range(n):
        ki_c = ki_ref[0, rows(c), :]
        acc = jnp.zeros((KC, TQ), F32)
        for h in range(N_IDX_HEADS):
            r = _mm(ki_c, qit_ref[0, h * IDX_DIM:(h + 1) * IDX_DIM, :])
            acc = acc + jnp.maximum(r, 0.0) * wt_ref[0, h:h + 1, :]
        bits = lax.bitcast_convert_type(acc * idx_scale, I32)
        key = jnp.where(bits < 0, bits ^ 0x7FFFFFFF, bits)
        if c == n - 1:
            key = jnp.where(causal, key, INT_MIN)
        key_scr[rows(c), :] = key
        hi_scr[rows(c), :] = (key >> 16).astype(I16)

    if n * KC <= top_k:
        for c in range(n):
            madd_scr[rows(c), :] = jnp.where(key_scr[rows(c), :] != INT_MIN, 0.0, -jnp.inf)
    else:
        def count16(ref, pred, t16):
            parts = []
            for c in range(n):
                hit = pred(ref[rows(c), :], t16).astype(I16)
                parts += [hit[r * PACK16:(r + 1) * PACK16] for r in range(KC // PACK16)]
            while len(parts) > 1:
                parts = [a + b for a, b in zip(parts[0::2], parts[1::2])] + parts[len(parts) & ~1:]
            return parts[0].astype(I32).sum(axis=0, keepdims=True)

        def search16(ref, want):
            def bit_body(it, t_u):
                cand = t_u | lax.shift_left(jnp.int32(1), 15 - it)
                cnt = count16(ref, lambda v, t_: v >= t_, (cand - HALF16).astype(I16))
                return jnp.where(cnt >= want, cand, t_u)
            return lax.fori_loop(0, 16, bit_body, jnp.zeros((1, TQ), I32)) - HALF16

        t_hi = search16(hi_scr, top_k)
        above = count16(hi_scr, lambda v, t_: v > t_, t_hi.astype(I16))
        for c in range(n):
            k = key_scr[rows(c), :]
            lo = jnp.where((k >> 16) == t_hi, (k & 0xFFFF) - HALF16, -HALF16)
            lo_scr[rows(c), :] = lo.astype(I16)
        t_lo = search16(lo_scr, top_k - above)
        thr = lax.shift_left(t_hi, 16) | (t_lo + HALF16)

        def count(pred, t_):
            acc = jnp.zeros((SUBLANES, TQ), I32)
            for c in range(n):
                acc = acc + _colsum8(pred(key_scr[rows(c), :], t_).astype(I32))
            return acc.sum(axis=0, keepdims=True)

        need = (top_k - count(lambda k, t_: k > t_, thr)).astype(F32)
        tri = (lax.broadcasted_iota(I32, (KC, KC), 1)
               < lax.broadcasted_iota(I32, (KC, KC), 0)).astype(_MXU_DTYPE)
        seen = jnp.zeros((1, TQ), F32)
        for c in range(n):
            k = key_scr[rows(c), :]
            eq = (k == thr).astype(F32)
            before = _mm(tri, eq.astype(_MXU_DTYPE)) + seen
            sel = ((k > thr) | ((eq > 0.0) & (before < need))) & (k != INT_MIN)
            madd_scr[rows(c), :] = jnp.where(sel, 0.0, -jnp.inf)
            seen = seen + _colsum8(eq).sum(axis=0, keepdims=True)

    pair_row = lax.broadcasted_iota(I32, (2 * HEAD_DIM, TQ), 0)
    ones_rows = jnp.ones((ONES_ROWS, KC), _MXU_DTYPE)
    n_far = max(n - 2, 0)

    def scores(p, slot):
        prow = pl.ds(pl.multiple_of(p * 2 * HEAD_DIM, 2 * HEAD_DIM), 2 * HEAD_DIM)
        qt = qt_ref[0, prow, :]
        zero = jnp.zeros_like(qt)
        q2 = jnp.concatenate([jnp.where(pair_row < HEAD_DIM, qt, zero),
                              jnp.where(pair_row >= HEAD_DIM, qt, zero)], axis=1)
        fars = [far_ref[2 * p + half] * LOG2E for half in range(2)]
        m_far = [jnp.full((SUBLANES, TQ), -jnp.inf, F32)] * 2
        m_near = list(m_far)
        for c in range(n):
            s2 = _mm(k_ref[0, p, rows(c), :], q2)
            for half in range(2):
                s = s2[:, half * TQ:(half + 1) * TQ] + madd_scr[rows(c), :]
                if c < n_far:
                    m_far[half] = jnp.maximum(m_far[half], _colmax8(s))
                else:
                    s = s + tile_ref[n - 1 - c, 2 * p + half]
                    m_near[half] = jnp.maximum(m_near[half], _colmax8(s))
                logit_scr[2 * slot + half, rows(c), :] = s
        return [jnp.maximum(m_far[half] + fars[half], m_near[half]).max(axis=0, keepdims=True)
                for half in range(2)]

    def weighted_values(p, slot, m):
        prow = pl.ds(pl.multiple_of(p * 2 * HEAD_DIM, 2 * HEAD_DIM), 2 * HEAD_DIM)
        fars = [far_ref[2 * p + half] * LOG2E for half in range(2)]
        acc = jnp.zeros((2 * HEAD_DIM + ONES_ROWS, 2 * TQ), F32)
        for c in range(n):
            e2 = jnp.concatenate(
                [jnp.exp2(logit_scr[2 * slot + half, rows(c), :]
                          - (m[half] - fars[half] if c < n_far else m[half])).astype(_MXU_DTYPE)
                 for half in range(2)], axis=1)
            lhs = jnp.concatenate([vt_ref[0, prow, rows(c)], ones_rows], axis=0)
            acc = acc + _mm(lhs, e2)
        for half in range(2):
            cols = slice(half * TQ, (half + 1) * TQ)
            ot_scr[pl.ds(pl.multiple_of((2 * p + half) * HEAD_DIM, HEAD_DIM), HEAD_DIM), :] = (
                acc[half * HEAD_DIM:(half + 1) * HEAD_DIM, cols]
                / acc[2 * HEAD_DIM:2 * HEAD_DIM + 1, cols])

    def group_body(g, carry):
        ms = [scores(g * PAIRS_PER_STEP + u, u) for u in range(PAIRS_PER_STEP)]
        for u in range(PAIRS_PER_STEP):
            weighted_values(g * PAIRS_PER_STEP + u, u, ms[u])
        return carry

    lax.fori_loop(0, N_HEADS // 2 // PAIRS_PER_STEP, group_body, 0)


def _dsa_kernel(far_ref, x_ref, qt_ref, qit_ref, wt_ref, k_ref, ki_ref, vt_ref, tile_ref, wout_ref,
                o_ref, key_scr, hi_scr, lo_scr, madd_scr, logit_scr, ot_scr, *, top_k):
    i = pl.program_id(1)
    for n in range(1, k_ref.shape[2] // KC + 1):
        pl.when(i == n - 1)(functools.partial(
            _dsa_block, n, far_ref, qt_ref, qit_ref, wt_ref, k_ref, ki_ref, vt_ref, tile_ref,
            key_scr, hi_scr, lo_scr, madd_scr, logit_scr, ot_scr, top_k=top_k))
    o_ref[0] = x_ref[0] + _mm_tn(ot_scr[...].astype(_MXU_DTYPE), wout_ref[...])


def _dsa(x, far, qt, qit, wt, k, ki, vt, tiles, wout, top_k):
    b, s, d = x.shape
    aw = N_HEADS * HEAD_DIM
    iw = N_IDX_HEADS * IDX_DIM
    return pl.pallas_call(
        functools.partial(_dsa_kernel, top_k=top_k),
        grid=(b, s // TQ),
        in_specs=[
            pl.BlockSpec(memory_space=pltpu.SMEM),
            pl.BlockSpec((1, TQ, d), lambda bi, i: (bi, i, 0)),
            pl.BlockSpec((1, aw, TQ), lambda bi, i: (bi, 0, i)),
            pl.BlockSpec((1, iw, TQ), lambda bi, i: (bi, 0, i)),
            pl.BlockSpec((1, N_IDX_HEADS, TQ), lambda bi, i: (bi, 0, i)),
            pl.BlockSpec((1, aw // LANES, s, LANES), lambda bi, i: (bi, 0, 0, 0)),
            pl.BlockSpec((1, s, IDX_DIM), lambda bi, i: (bi, 0, 0)),
            pl.BlockSpec((1, aw, s), lambda bi, i: (bi, 0, 0)),
            _full(tiles.shape), _full(wout.shape),
        ],
        out_specs=pl.BlockSpec((1, TQ, d), lambda bi, i: (bi, i, 0)),
        out_shape=jax.ShapeDtypeStruct((b, s, d), F32),
        scratch_shapes=[
            pltpu.VMEM((s, TQ), I32),
            pltpu.VMEM((s, TQ), I16),
            pltpu.VMEM((s, TQ), I16),
            pltpu.VMEM((s, TQ), F32),
            pltpu.VMEM((2 * PAIRS_PER_STEP, s, TQ), F32),
            pltpu.VMEM((aw, TQ), F32),
        ],
        compiler_params=_params(),
        name="dsa",
    )(far, x, qt, qit, wt, k, ki, vt, tiles, wout)


def kernel(x, ffn1_norm, ffn1_w_gate, ffn1_w_up, ffn1_w_down, mix_norm, ffn2_norm, ffn2_w_gate,
           ffn2_w_up, ffn2_w_down, ab_w_in, pool_w, pool_scale, conv_w, conv_b, conv_ln_g,
           conv_ln_b, ab_w_out, c_w_in, c_w_out, rel_bias, final_norm):
    b, s, d = x.shape
    depth = ffn1_norm.shape[0]
    md = _MXU_DTYPE
    aw = N_HEADS * HEAD_DIM
    iw = N_IDX_HEADS * IDX_DIM
    top_k = min(TOPK_MAX, s // 4)
    assert s % MIX_TOKENS == 0 and s % CIN_TOKENS == 0 and (b * s) % FFN_TOKENS == 0 and s % TQ == 0 and top_k <= KC

    row = lambda v: v.reshape(1, -1)

    if depth > 1:
        sk = jnp.arange(KC, dtype=I32)[:, None]
        tq = jnp.arange(TQ, dtype=I32)[None, :]
        bkt = jnp.stack([_t5_bucket(jnp.maximum(tq + dl * TQ - sk, 0)) for dl in (0, 1)])
        tiles = _bias_tiles(rel_bias, bkt)
        assert 2 * TQ - KC + 1 >= REL_MAX_DIST
        far = rel_bias[REL_BUCKETS - 1]

    ffn1 = [_to_mxu(w_) for w_ in (ffn1_w_gate, ffn1_w_up, ffn1_w_down)]
    ffn2 = [_to_mxu(w_) for w_ in (ffn2_w_gate, ffn2_w_up, ffn2_w_down)]
    for layer in range(depth):
        i = layer // 2
        last = layer == depth - 1
        x2 = _ffn(x.reshape(b * s, d), row(ffn1_norm[layer]), *ffn1, row(final_norm), False, layer)
        x = x2.reshape(b, s, d)
        if layer % 2 == 0:
            x = _ab_mixer(x, row(mix_norm[layer]), ab_w_in[i].astype(md), pool_w[i].astype(md),
                          row(pool_scale[i]),
                          jnp.broadcast_to(conv_w[i][:, None, :], (CONV_KERNEL, SUBLANES, conv_w.shape[-1])),
                          row(conv_b[i]), row(conv_ln_g[i]),
                          row(conv_ln_b[i]), ab_w_out[i].astype(md))
        else:
            w = c_w_in[i]
            wq, wk, wv = w[:, 0:aw], w[:, aw:2 * aw], w[:, 2 * aw:3 * aw]
            wqi = w[:, 3 * aw:3 * aw + iw]
            wki = w[:, 3 * aw + iw:3 * aw + iw + IDX_DIM]
            www = w[:, 3 * aw + iw + IDX_DIM:]
            wn = jnp.concatenate([wk, wki], axis=1).astype(md)
            wt = jnp.concatenate([wq, wv, wqi], axis=1).T.astype(md)
            k, ki, qt, vt, qit, wts = _c_inproj(x, row(mix_norm[layer]), wn, wt, www.T.astype(md))
            x = _dsa(x, far, qt, qit, wts, k, ki, vt, tiles, c_w_out[i].astype(md), top_k)
        x2 = _ffn(x.reshape(b * s, d), row(ffn2_norm[layer]), *ffn2, row(final_norm), last, layer)
        x = x2.reshape(b, s, d)
    return x
```

```python
import functools
import math

import jax
import jax.numpy as jnp
from jax import lax
from jax.experimental import pallas as pl
from jax.experimental.pallas import tpu as pltpu

F32 = jnp.float32
I32 = jnp.int32
I16 = jnp.int16
_MXU_DTYPE = jnp.bfloat16

NORM_EPS = 1e-6
FFN_RES = 0.5
POOL_WINDOWS = (2, 4, 8, 16)
POOL_GROUP = 128
CONV_KERNEL = 31
N_HEADS = 16
HEAD_DIM = 64
N_IDX_HEADS = 8
IDX_DIM = 64
TOPK_MAX = 256
REL_BUCKETS = 32
REL_MAX_DIST = 128

V7X_VMEM_BYTES = 64 * 1024 * 1024
VMEM_LIMIT = V7X_VMEM_BYTES - 6 * 1024 * 1024
SUBLANES = 8
LANES = 128

FFN_TOKENS = 512
FFN_ROW_GROUPS = 4
MIX_TOKENS = 512
CIN_TOKENS = 1024
CIN_ROW_GROUPS = 4
CONV_ROWS = 32
CAST_BLOCK_BYTES = 6 * 1024 * 1024
HALO = 32
TQ = 256
KC = 256
INT_MIN = -2 ** 31
HALF16 = 2 ** 15
PACK16 = 16
LOG2E = 1.0 / math.log(2.0)
PAIRS_PER_STEP = 2
ONES_ROWS = 16


def _rms(x, g):
    return x * lax.rsqrt(jnp.mean(x * x, axis=-1, keepdims=True) + NORM_EPS) * g


def _mm(a, b):
    return jnp.dot(a, b, preferred_element_type=F32)


def _mm_nt(a, b):
    return lax.dot_general(a, b, (((1,), (1,)), ((), ())), preferred_element_type=F32)


def _mm_tn(a, b):
    return lax.dot_general(a, b, (((0,), (0,)), ((), ())), preferred_element_type=F32)


def _params():
    return pltpu.CompilerParams(vmem_limit_bytes=VMEM_LIMIT)


def _full(shape):
    n = len(shape)
    return pl.BlockSpec(shape, lambda *_: (0,) * n)


def _cast_kernel(w_ref, o_ref):
    o_ref[...] = w_ref[...].astype(o_ref.dtype)


def _to_mxu(w):
    rows, cols = w.shape[-2], w.shape[-1]
    rb = rows
    while rb * cols * w.dtype.itemsize > CAST_BLOCK_BYTES and rb % 2 == 0 and (rb // 2) % (2 * SUBLANES) == 0:
        rb //= 2
    return pl.pallas_call(
        _cast_kernel,
        grid=(w.shape[0], rows // rb),
        in_specs=[pl.BlockSpec((1, rb, cols), lambda l, r: (l, r, 0))],
        out_specs=pl.BlockSpec((1, rb, cols), lambda l, r: (l, r, 0)),
        out_shape=jax.ShapeDtypeStruct(w.shape, _MXU_DTYPE),
        compiler_params=_params(),
        name="cast",
    )(w)


def _ffn_kernel(x_ref, g_ref, wg_ref, wu_ref, wd_ref, fg_ref, o_ref, *, final_norm):
    t = x_ref.shape[0]
    for r0 in range(0, t, t // FFN_ROW_GROUPS):
        rows = slice(r0, r0 + t // FFN_ROW_GROUPS)
        x = x_ref[rows, :]
        h = _rms(x, g_ref[...]).astype(_MXU_DTYPE)
        gate = _mm(h, wg_ref[...])
        up = _mm(h, wu_ref[...])
        act = (gate * jax.nn.sigmoid(gate) * up).astype(_MXU_DTYPE)
        y = x + FFN_RES * _mm(act, wd_ref[...])
        if final_norm:
            y = _rms(y, fg_ref[...])
        o_ref[rows, :] = y


def _ffn(x2, g, wg, wu, wd, fg, final_norm, layer):
    n, d = x2.shape
    f = wg.shape[2]
    t = FFN_TOKENS
    of_layer = lambda rows, cols: pl.BlockSpec((None, rows, cols), lambda i: (layer, 0, 0))
    return pl.pallas_call(
        functools.partial(_ffn_kernel, final_norm=final_norm),
        grid=(n // t,),
        in_specs=[
            pl.BlockSpec((t, d), lambda i: (i, 0)),
            _full((1, d)), of_layer(d, f), of_layer(d, f), of_layer(f, d), _full((1, d)),
        ],
        out_specs=pl.BlockSpec((t, d), lambda i: (i, 0)),
        out_shape=jax.ShapeDtypeStruct((n, d), F32),
        compiler_params=_params(),
        name="ffn_final" if final_norm else "ffn",
    )(x2, g, wg, wu, wd, fg)


def _ab_kernel(x_ref, g_ref, win_ref, pw_ref, ps_ref, cw_ref, cb_ref, lg_ref, lb_ref, wout_ref,
               o_ref, ubuf, zbuf, zsh, ycat, conv_scr):
    t = x_ref.shape[1]
    pw_width = ps_ref.shape[1]
    cw_width = cb_ref.shape[1]
    j = pl.program_id(1)

    @pl.when(j == 0)
    def _():
        ubuf[0:HALO, :] = jnp.zeros((HALO, pw_width), F32)
        zbuf[0:HALO, :] = jnp.zeros((HALO, cw_width), F32)

    @pl.when(j > 0)
    def _():
        ubuf[0:HALO, :] = ubuf[t:t + HALO, :]
        zbuf[0:HALO, :] = zbuf[t:t + HALO, :]

    x = x_ref[0]
    h = _rms(x, g_ref[...]).astype(_MXU_DTYPE)
    proj = _mm(h, win_ref[...])
    u = proj[:, 0:pw_width]
    val = proj[:, pw_width:pw_width + cw_width]
    gate = proj[:, pw_width + cw_width:pw_width + 2 * cw_width]
    ubuf[HALO:HALO + t, :] = u
    zbuf[HALO:HALO + t, :] = val * jax.nn.sigmoid(gate)
    for r in range(1, SUBLANES):
        zsh[r - 1, SUBLANES:HALO + t, :] = zbuf[SUBLANES - r:HALO + t - r, :]

    pos = j * t + lax.broadcasted_iota(I32, (t, POOL_GROUP), 0)
    for g, w in enumerate(POOL_WINDOWS):
        lo, hi = g * POOL_GROUP, (g + 1) * POOL_GROUP
        ug = ubuf[HALO:HALO + t, lo:hi]
        acc = ug
        for back in range(1, w):
            acc = acc + ubuf[HALO - back:HALO - back + t, lo:hi]
        count = jnp.minimum(pos + 1, w).astype(F32)
        diff = (acc / count - ug).astype(_MXU_DTYPE)
        ycat[:, lo:hi] = (_mm(diff, pw_ref[g]) * ps_ref[:, lo:hi]).astype(ycat.dtype)

    def conv_block(rb, carry):
        base = rb * CONV_ROWS
        groups = CONV_ROWS // SUBLANES
        acc = jnp.broadcast_to(cb_ref[...].reshape(1, 1, cw_width), (groups, SUBLANES, cw_width))
        for k in range(CONV_KERNEL):
            a, r = divmod(CONV_KERNEL - 1 - k, SUBLANES)
            start = pl.multiple_of(base + (HALO - a * SUBLANES), SUBLANES)
            src = zbuf if r == 0 else zsh.at[r - 1]
            win = src[pl.ds(start, CONV_ROWS), :].reshape(groups, SUBLANES, cw_width)
            acc = acc + win * cw_ref[k][None]
        conv_scr[pl.ds(pl.multiple_of(base, CONV_ROWS), CONV_ROWS), :] = acc.reshape(CONV_ROWS, cw_width)
        return carry

    lax.fori_loop(0, t // CONV_ROWS, conv_block, 0)
    conv = conv_scr[...]
    mu = jnp.mean(conv, axis=-1, keepdims=True)
    cen = conv - mu
    var = jnp.mean(cen * cen, axis=-1, keepdims=True)
    zn = cen * lax.rsqrt(var + NORM_EPS) * lg_ref[...] + lb_ref[...]
    ycat[:, pw_width:pw_width + cw_width] = (zn * jax.nn.sigmoid(zn)).astype(ycat.dtype)

    o_ref[0] = x + _mm(ycat[...], wout_ref[...])


def _ab_mixer(x, g, win, pw, ps, cw, cb, lg, lb, wout):
    b, s, d = x.shape
    t = MIX_TOKENS
    pwid, cwid = ps.shape[1], cb.shape[1]
    return pl.pallas_call(
        _ab_kernel,
        grid=(b, s // t),
        in_specs=[
            pl.BlockSpec((1, t, d), lambda bi, j: (bi, j, 0)),
            _full((1, d)), _full(win.shape), _full(pw.shape), _full(ps.shape), _full(cw.shape),
            _full(cb.shape), _full(lg.shape), _full(lb.shape), _full(wout.shape),
        ],
        out_specs=pl.BlockSpec((1, t, d), lambda bi, j: (bi, j, 0)),
        out_shape=jax.ShapeDtypeStruct((b, s, d), F32),
        scratch_shapes=[
            pltpu.VMEM((HALO + t, pwid), F32),
            pltpu.VMEM((HALO + t, cwid), F32),
            pltpu.VMEM((SUBLANES - 1, HALO + t, cwid), F32),
            pltpu.VMEM((t, pwid + cwid), _MXU_DTYPE),
            pltpu.VMEM((t, cwid), F32),
        ],
        compiler_params=_params(),
        name="ab_mixer",
    )(x, g, win, pw, ps, cw, cb, lg, lb, wout)


def _cin_kernel(x_ref, g_ref, wn_ref, wt_ref, ww_ref, k_out, ki_out, qt_out, vt_out, qit_out, wt_out):
    t = x_ref.shape[1]
    aw = qt_out.shape[1]
    iw = qit_out.shape[1]
    for r0 in range(0, t, t // CIN_ROW_GROUPS):
        rows = slice(r0, r0 + t // CIN_ROW_GROUPS)
        h = _rms(x_ref[0, rows, :], g_ref[...]).astype(_MXU_DTYPE)
        nat = _mm(h, wn_ref[...])
        for p in range(aw // LANES):
            k_out[0, p, rows, :] = nat[:, p * LANES:(p + 1) * LANES].astype(k_out.dtype)
        ki_out[0, rows, :] = nat[:, aw:aw + IDX_DIM].astype(ki_out.dtype)
        tr = _mm_nt(wt_ref[...], h)
        qt_out[0, :, rows] = (tr[0:aw] * (HEAD_DIM ** -0.5 * LOG2E)).astype(qt_out.dtype)
        vt_out[0, :, rows] = tr[aw:2 * aw].astype(vt_out.dtype)
        qit_out[0, :, rows] = tr[2 * aw:2 * aw + iw].astype(qit_out.dtype)
        wt_out[0, :, rows] = _mm_nt(ww_ref[...], h)


def _c_inproj(x, g, wn, wt, ww):
    b, s, d = x.shape
    t = CIN_TOKENS
    aw = N_HEADS * HEAD_DIM
    iw = N_IDX_HEADS * IDX_DIM
    md = _MXU_DTYPE
    return pl.pallas_call(
        _cin_kernel,
        grid=(b, s // t),
        in_specs=[
            pl.BlockSpec((1, t, d), lambda bi, j: (bi, j, 0)),
            _full((1, d)), _full(wn.shape), _full(wt.shape), _full(ww.shape),
        ],
        out_specs=[
            pl.BlockSpec((1, aw // LANES, t, LANES), lambda bi, j: (bi, 0, j, 0)),
            pl.BlockSpec((1, t, IDX_DIM), lambda bi, j: (bi, j, 0)),
            pl.BlockSpec((1, aw, t), lambda bi, j: (bi, 0, j)),
            pl.BlockSpec((1, aw, t), lambda bi, j: (bi, 0, j)),
            pl.BlockSpec((1, iw, t), lambda bi, j: (bi, 0, j)),
            pl.BlockSpec((1, N_IDX_HEADS, t), lambda bi, j: (bi, 0, j)),
        ],
        out_shape=[
            jax.ShapeDtypeStruct((b, aw // LANES, s, LANES), md),
            jax.ShapeDtypeStruct((b, s, IDX_DIM), md),
            jax.ShapeDtypeStruct((b, aw, s), md),
            jax.ShapeDtypeStruct((b, aw, s), md),
            jax.ShapeDtypeStruct((b, iw, s), md),
            jax.ShapeDtypeStruct((b, N_IDX_HEADS, s), F32),
        ],
        compiler_params=_params(),
        name="c_inproj",
    )(x, g, wn, wt, ww)


def _bias_kernel(rel_ref, bkt_ref, o_ref):
    h = pl.program_id(1)
    bkt = bkt_ref[0]
    acc = jnp.zeros(bkt.shape, F32)
    for b in range(REL_BUCKETS):
        acc = jnp.where(bkt == b, rel_ref[b, h], acc)
    o_ref[0, 0] = acc * LOG2E


def _bias_tiles(rel_bias, bkt):
    nd = bkt.shape[0]
    return pl.pallas_call(
        _bias_kernel,
        grid=(nd, N_HEADS),
        in_specs=[
            pl.BlockSpec(memory_space=pltpu.SMEM),
            pl.BlockSpec((1, KC, TQ), lambda dl, h: (dl, 0, 0)),
        ],
        out_specs=pl.BlockSpec((1, 1, KC, TQ), lambda dl, h: (dl, h, 0, 0)),
        out_shape=jax.ShapeDtypeStruct((nd, N_HEADS, KC, TQ), F32),
        name="bias_tiles",
    )(rel_bias, bkt)


def _t5_bucket(dist):
    max_exact = REL_BUCKETS // 2
    d = jnp.maximum(dist, max_exact).astype(F32)
    large = max_exact + (jnp.log(d / max_exact) / math.log(REL_MAX_DIST / max_exact)
                         * (REL_BUCKETS - max_exact)).astype(I32)
    large = jnp.minimum(large, REL_BUCKETS - 1)
    return jnp.where(dist < max_exact, dist, large)


def _colsum8(v):
    return v.reshape(v.shape[0] // SUBLANES, SUBLANES, v.shape[1]).sum(axis=0)


def _colmax8(v):
    return v.reshape(v.shape[0] // SUBLANES, SUBLANES, v.shape[1]).max(axis=0)


def _dsa_block(n, far_ref, qt_ref, qit_ref, wt_ref, k_ref, ki_ref, vt_ref, tile_ref,
               key_scr, hi_scr, lo_scr, madd_scr, logit_scr, ot_scr, *, top_k):
    idx_scale = (N_IDX_HEADS * IDX_DIM) ** -0.5
    causal = lax.broadcasted_iota(I32, (KC, TQ), 0) <= lax.broadcasted_iota(I32, (KC, TQ), 1)
    rows = lambda c: slice(c * KC, (c + 1) * KC)

    for c in range(n):
        ki_c = ki_ref[0, rows(c), :]
        acc = jnp.zeros((KC, TQ), F32)
        for h in range(N_IDX_HEADS):
            r = _mm(ki_c, qit_ref[0, h * IDX_DIM:(h + 1) * IDX_DIM, :])
            acc = acc + jnp.maximum(r, 0.0) * wt_ref[0, h:h + 1, :]
        bits = lax.bitcast_convert_type(acc * idx_scale, I32)
        key = jnp.where(bits < 0, bits ^ 0x7FFFFFFF, bits)
        if c == n - 1:
            key = jnp.where(causal, key, INT_MIN)
        key_scr[rows(c), :] = key
        hi_scr[rows(c), :] = (key >> 16).astype(I16)

    if n * KC <= top_k:
        for c in range(n):
            madd_scr[rows(c), :] = jnp.where(key_scr[rows(c), :] != INT_MIN, 0.0, -jnp.inf)
    else:
        def count16(ref, pred, t16):
            parts = []
            for c in range(n):
                hit = pred(ref[rows(c), :], t16).astype(I16)
                parts += [hit[r * PACK16:(r + 1) * PACK16] for r in range(KC // PACK16)]
            while len(parts) > 1:
                parts = [a + b for a, b in zip(parts[0::2], parts[1::2])] + parts[len(parts) & ~1:]
            return parts[0].astype(I32).sum(axis=0, keepdims=True)

        def search16(ref, want):
            def bit_body(it, t_u):
                cand = t_u | lax.shift_left(jnp.int32(1), 15 - it)
                cnt = count16(ref, lambda v, t_: v >= t_, (cand - HALF16).astype(I16))
                return jnp.where(cnt >= want, cand, t_u)
            return lax.fori_loop(0, 16, bit_body, jnp.zeros((1, TQ), I32)) - HALF16

        t_hi = search16(hi_scr, top_k)
        above = count16(hi_scr, lambda v, t_: v > t_, t_hi.astype(I16))
        for c in range(n):
            k = key_scr[rows(c), :]
            lo = jnp.where((k >> 16) == t_hi, (k & 0xFFFF) - HALF16, -HALF16)
            lo_scr[rows(c), :] = lo.astype(I16)
        t_lo = search16(lo_scr, top_k - above)
        thr = lax.shift_left(t_hi, 16) | (t_lo + HALF16)

        above_t = above + count16(lo_scr, lambda v, t_: v > t_, t_lo.astype(I16))
        need = (top_k - above_t).astype(F32)
        tri = (lax.broadcasted_iota(I32, (KC, KC), 1)
               < lax.broadcasted_iota(I32, (KC, KC), 0)).astype(_MXU_DTYPE)
        seen = jnp.zeros((1, TQ), F32)
        for c in range(n):
            k = key_scr[rows(c), :]
            eq = (k == thr).astype(F32)
            before = _mm(tri, eq.astype(_MXU_DTYPE)) + seen
            sel = ((k > thr) | ((eq > 0.0) & (before < need))) & (k != INT_MIN)
            madd_scr[rows(c), :] = jnp.where(sel, 0.0, -jnp.inf)
            seen = seen + _colsum8(eq).sum(axis=0, keepdims=True)

    pair_row = lax.broadcasted_iota(I32, (2 * HEAD_DIM, TQ), 0)
    ones_rows = jnp.ones((ONES_ROWS, KC), _MXU_DTYPE)
    n_far = max(n - 2, 0)

    def scores(p, slot):
        prow = pl.ds(pl.multiple_of(p * 2 * HEAD_DIM, 2 * HEAD_DIM), 2 * HEAD_DIM)
        qt = qt_ref[0, prow, :]
        zero = jnp.zeros_like(qt)
        q2 = jnp.concatenate([jnp.where(pair_row < HEAD_DIM, qt, zero),
                              jnp.where(pair_row >= HEAD_DIM, qt, zero)], axis=1)
        fars = [far_ref[2 * p + half] * LOG2E for half in range(2)]
        m_far = [jnp.full((SUBLANES, TQ), -jnp.inf, F32)] * 2
        m_near = list(m_far)
        for c in range(n):
            s2 = _mm(k_ref[0, p, rows(c), :], q2)
            for half in range(2):
                s = s2[:, half * TQ:(half + 1) * TQ] + madd_scr[rows(c), :]
                if c < n_far:
                    m_far[half] = jnp.maximum(m_far[half], _colmax8(s))
                else:
                    s = s + tile_ref[n - 1 - c, 2 * p + half]
                    m_near[half] = jnp.maximum(m_near[half], _colmax8(s))
                logit_scr[2 * slot + half, rows(c), :] = s
        return [jnp.maximum(m_far[half] + fars[half], m_near[half]).max(axis=0, keepdims=True)
                for half in range(2)]

    def weighted_values(p, slot, m):
        prow = pl.ds(pl.multiple_of(p * 2 * HEAD_DIM, 2 * HEAD_DIM), 2 * HEAD_DIM)
        fars = [far_ref[2 * p + half] * LOG2E for half in range(2)]
        acc = jnp.zeros((2 * HEAD_DIM + ONES_ROWS, 2 * TQ), F32)
        for c in range(n):
            e2 = jnp.concatenate(
                [jnp.exp2(logit_scr[2 * slot + half, rows(c), :]
                          - (m[half] - fars[half] if c < n_far else m[half])).astype(_MXU_DTYPE)
                 for half in range(2)], axis=1)
            lhs = jnp.concatenate([vt_ref[0, prow, rows(c)], ones_rows], axis=0)
            acc = acc + _mm(lhs, e2)
        for half in range(2):
            cols = slice(half * TQ, (half + 1) * TQ)
            ot_scr[pl.ds(pl.multiple_of((2 * p + half) * HEAD_DIM, HEAD_DIM), HEAD_DIM), :] = (
                acc[half * HEAD_DIM:(half + 1) * HEAD_DIM, cols]
                / acc[2 * HEAD_DIM:2 * HEAD_DIM + 1, cols])

    def group_body(g, carry):
        ms = [scores(g * PAIRS_PER_STEP + u, u) for u in range(PAIRS_PER_STEP)]
        for u in range(PAIRS_PER_STEP):
            weighted_values(g * PAIRS_PER_STEP + u, u, ms[u])
        return carry

    lax.fori_loop(0, N_HEADS // 2 // PAIRS_PER_STEP, group_body, 0)


def _dsa_kernel(far_ref, x_ref, qt_ref, qit_ref, wt_ref, k_ref, ki_ref, vt_ref, tile_ref, wout_ref,
                o_ref, key_scr, hi_scr, lo_scr, madd_scr, logit_scr, ot_scr, *, top_k):
    i = pl.program_id(1)
    for n in range(1, k_ref.shape[2] // KC + 1):
        pl.when(i == n - 1)(functools.partial(
            _dsa_block, n, far_ref, qt_ref, qit_ref, wt_ref, k_ref, ki_ref, vt_ref, tile_ref,
            key_scr, hi_scr, lo_scr, madd_scr, logit_scr, ot_scr, top_k=top_k))
    o_ref[0] = x_ref[0] + _mm_tn(ot_scr[...].astype(_MXU_DTYPE), wout_ref[...])


def _dsa(x, far, qt, qit, wt, k, ki, vt, tiles, wout, top_k):
    b, s, d = x.shape
    aw = N_HEADS * HEAD_DIM
    iw = N_IDX_HEADS * IDX_DIM
    return pl.pallas_call(
        functools.partial(_dsa_kernel, top_k=top_k),
        grid=(b, s // TQ),
        in_specs=[
            pl.BlockSpec(memory_space=pltpu.SMEM),
            pl.BlockSpec((1, TQ, d), lambda bi, i: (bi, i, 0)),
            pl.BlockSpec((1, aw, TQ), lambda bi, i: (bi, 0, i)),
            pl.BlockSpec((1, iw, TQ), lambda bi, i: (bi, 0, i)),
            pl.BlockSpec((1, N_IDX_HEADS, TQ), lambda bi, i: (bi, 0, i)),
            pl.BlockSpec((1, aw // LANES, s, LANES), lambda bi, i: (bi, 0, 0, 0)),
            pl.BlockSpec((1, s, IDX_DIM), lambda bi, i: (bi, 0, 0)),
            pl.BlockSpec((1, aw, s), lambda bi, i: (bi, 0, 0)),
            _full(tiles.shape), _full(wout.shape),
        ],
        out_specs=pl.BlockSpec((1, TQ, d), lambda bi, i: (bi, i, 0)),
        out_shape=jax.ShapeDtypeStruct((b, s, d), F32),
        scratch_shapes=[
            pltpu.VMEM((s, TQ), I32),
            pltpu.VMEM((s, TQ), I16),
            pltpu.VMEM((s, TQ), I16),
            pltpu.VMEM((s, TQ), F32),
            pltpu.VMEM((2 * PAIRS_PER_STEP, s, TQ), F32),
            pltpu.VMEM((aw, TQ), F32),
        ],
        compiler_params=_params(),
        name="dsa",
    )(far, x, qt, qit, wt, k, ki, vt, tiles, wout)


def kernel(x, ffn1_norm, ffn1_w_gate, ffn1_w_up, ffn1_w_down, mix_norm, ffn2_norm, ffn2_w_gate,
           ffn2_w_up, ffn2_w_down, ab_w_in, pool_w, pool_scale, conv_w, conv_b, conv_ln_g,
           conv_ln_b, ab_w_out, c_w_in, c_w_out, rel_bias, final_norm):
    b, s, d = x.shape
    depth = ffn1_norm.shape[0]
    md = _MXU_DTYPE
    aw = N_HEADS * HEAD_DIM
    iw = N_IDX_HEADS * IDX_DIM
    top_k = min(TOPK_MAX, s // 4)
    assert s % MIX_TOKENS == 0 and s % CIN_TOKENS == 0 and (b * s) % FFN_TOKENS == 0 and s % TQ == 0 and top_k <= KC

    row = lambda v: v.reshape(1, -1)

    if depth > 1:
        sk = jnp.arange(KC, dtype=I32)[:, None]
        tq = jnp.arange(TQ, dtype=I32)[None, :]
        bkt = jnp.stack([_t5_bucket(jnp.maximum(tq + dl * TQ - sk, 0)) for dl in (0, 1)])
        tiles = _bias_tiles(rel_bias, bkt)
        assert 2 * TQ - KC + 1 >= REL_MAX_DIST
        far = rel_bias[REL_BUCKETS - 1]

    ffn1 = [_to_mxu(w_) for w_ in (ffn1_w_gate, ffn1_w_up, ffn1_w_down)]
    ffn2 = [_to_mxu(w_) for w_ in (ffn2_w_gate, ffn2_w_up, ffn2_w_down)]
    for layer in range(depth):
        i = layer // 2
        last = layer == depth - 1
        x2 = _ffn(x.reshape(b * s, d), row(ffn1_norm[layer]), *ffn1, row(final_norm), False, layer)
        x = x2.reshape(b, s, d)
        if layer % 2 == 0:
            x = _ab_mixer(x, row(mix_norm[layer]), ab_w_in[i].astype(md), pool_w[i].astype(md),
                          row(pool_scale[i]),
                          jnp.broadcast_to(conv_w[i][:, None, :], (CONV_KERNEL, SUBLANES, conv_w.shape[-1])),
                          row(conv_b[i]), row(conv_ln_g[i]),
                          row(conv_ln_b[i]), ab_w_out[i].astype(md))
        else:
            w = c_w_in[i]
            wq, wk, wv = w[:, 0:aw], w[:, aw:2 * aw], w[:, 2 * aw:3 * aw]
            wqi = w[:, 3 * aw:3 * aw + iw]
            wki = w[:, 3 * aw + iw:3 * aw + iw + IDX_DIM]
            www = w[:, 3 * aw + iw + IDX_DIM:]
            wn = jnp.concatenate([wk, wki], axis=1).astype(md)
            wt = jnp.concatenate([wq, wv, wqi], axis=1).T.astype(md)
            k, ki, qt, vt, qit, wts = _c_inproj(x, row(mix_norm[layer]), wn, wt, www.T.astype(md))
            x = _dsa(x, far, qt, qit, wts, k, ki, vt, tiles, c_w_out[i].astype(md), top_k)
        x2 = _ffn(x.reshape(b * s, d), row(ffn2_norm[layer]), *ffn2, row(final_norm), last, layer)
        x = x2.reshape(b, s, d)
    return x
```

```python
import functools
import math

import jax
import jax.numpy as jnp
from jax import lax
from jax.experimental import pallas as pl
from jax.experimental.pallas import tpu as pltpu

F32 = jnp.float32
I32 = jnp.int32
I16 = jnp.int16
_MXU_DTYPE = jnp.bfloat16

NORM_EPS = 1e-6
FFN_RES = 0.5
POOL_WINDOWS = (2, 4, 8, 16)
POOL_GROUP = 128
CONV_KERNEL = 31
N_HEADS = 16
HEAD_DIM = 64
N_IDX_HEADS = 8
IDX_DIM = 64
TOPK_MAX = 256
REL_BUCKETS = 32
REL_MAX_DIST = 128

V7X_VMEM_BYTES = 64 * 1024 * 1024
VMEM_LIMIT = V7X_VMEM_BYTES - 6 * 1024 * 1024
SUBLANES = 8
LANES = 128

FFN_TOKENS = 1024
FFN_ROW_GROUPS = 8
MIX_TOKENS = 512
CIN_TOKENS = 1024
CIN_ROW_GROUPS = 4
CONV_ROWS = 32
CAST_BLOCK_BYTES = 6 * 1024 * 1024
HALO = 32
TQ = 256
KC = 256
INT_MIN = -2 ** 31
HALF16 = 2 ** 15
PACK16 = 16
LOG2E = 1.0 / math.log(2.0)
PAIRS_PER_STEP = 2
ONES_ROWS = 16


def _rms(x, g):
    return x * lax.rsqrt(jnp.mean(x * x, axis=-1, keepdims=True) + NORM_EPS) * g


def _mm(a, b):
    return jnp.dot(a, b, preferred_element_type=F32)


def _mm_nt(a, b):
    return lax.dot_general(a, b, (((1,), (1,)), ((), ())), preferred_element_type=F32)


def _mm_tn(a, b):
    return lax.dot_general(a, b, (((0,), (0,)), ((), ())), preferred_element_type=F32)


def _params():
    return pltpu.CompilerParams(vmem_limit_bytes=VMEM_LIMIT)


def _full(shape):
    n = len(shape)
    return pl.BlockSpec(shape, lambda *_: (0,) * n)


def _cast_kernel(w_ref, o_ref):
    o_ref[...] = w_ref[...].astype(o_ref.dtype)


def _to_mxu(w):
    rows, cols = w.shape[-2], w.shape[-1]
    rb = rows
    while rb * cols * w.dtype.itemsize > CAST_BLOCK_BYTES and rb % 2 == 0 and (rb // 2) % (2 * SUBLANES) == 0:
        rb //= 2
    return pl.pallas_call(
        _cast_kernel,
        grid=(w.shape[0], rows // rb),
        in_specs=[pl.BlockSpec((1, rb, cols), lambda l, r: (l, r, 0))],
        out_specs=pl.BlockSpec((1, rb, cols), lambda l, r: (l, r, 0)),
        out_shape=jax.ShapeDtypeStruct(w.shape, _MXU_DTYPE),
        compiler_params=_params(),
        name="cast",
    )(w)


def _ffn_kernel(x_ref, g_ref, wg_ref, wu_ref, wd_ref, fg_ref, o_ref, *, final_norm):
    t = x_ref.shape[0]
    for r0 in range(0, t, t // FFN_ROW_GROUPS):
        rows = slice(r0, r0 + t // FFN_ROW_GROUPS)
        x = x_ref[rows, :]
        h = _rms(x, g_ref[...]).astype(_MXU_DTYPE)
        gate = _mm(h, wg_ref[...])
        up = _mm(h, wu_ref[...])
        act = (gate * jax.nn.sigmoid(gate) * up).astype(_MXU_DTYPE)
        y = x + FFN_RES * _mm(act, wd_ref[...])
        if final_norm:
            y = _rms(y, fg_ref[...])
        o_ref[rows, :] = y


def _ffn(x2, g, wg, wu, wd, fg, final_norm, layer):
    n, d = x2.shape
    f = wg.shape[2]
    t = FFN_TOKENS
    of_layer = lambda rows, cols: pl.BlockSpec((None, rows, cols), lambda i: (layer, 0, 0))
    return pl.pallas_call(
        functools.partial(_ffn_kernel, final_norm=final_norm),
        grid=(n // t,),
        in_specs=[
            pl.BlockSpec((t, d), lambda i: (i, 0)),
            _full((1, d)), of_layer(d, f), of_layer(d, f), of_layer(f, d), _full((1, d)),
        ],
        out_specs=pl.BlockSpec((t, d), lambda i: (i, 0)),
        out_shape=jax.ShapeDtypeStruct((n, d), F32),
        compiler_params=_params(),
        name="ffn_final" if final_norm else "ffn",
    )(x2, g, wg, wu, wd, fg)


def _ab_kernel(x_ref, g_ref, win_ref, pw_ref, ps_ref, cw_ref, cb_ref, lg_ref, lb_ref, wout_ref,
               o_ref, ubuf, zbuf, zsh, ycat, conv_scr):
    t = x_ref.shape[1]
    pw_width = ps_ref.shape[1]
    cw_width = cb_ref.shape[1]
    j = pl.program_id(1)

    @pl.when(j == 0)
    def _():
        ubuf[0:HALO, :] = jnp.zeros((HALO, pw_width), F32)
        zbuf[0:HALO, :] = jnp.zeros((HALO, cw_width), F32)

    @pl.when(j > 0)
    def _():
        ubuf[0:HALO, :] = ubuf[t:t + HALO, :]
        zbuf[0:HALO, :] = zbuf[t:t + HALO, :]

    x = x_ref[0]
    h = _rms(x, g_ref[...]).astype(_MXU_DTYPE)
    proj = _mm(h, win_ref[...])
    u = proj[:, 0:pw_width]
    val = proj[:, pw_width:pw_width + cw_width]
    gate = proj[:, pw_width + cw_width:pw_width + 2 * cw_width]
    ubuf[HALO:HALO + t, :] = u
    zbuf[HALO:HALO + t, :] = val * jax.nn.sigmoid(gate)
    for r in range(1, SUBLANES):
        zsh[r - 1, SUBLANES:HALO + t, :] = zbuf[SUBLANES - r:HALO + t - r, :]

    pos = j * t + lax.broadcasted_iota(I32, (t, POOL_GROUP), 0)
    for g, w in enumerate(POOL_WINDOWS):
        lo, hi = g * POOL_GROUP, (g + 1) * POOL_GROUP
        ug = ubuf[HALO:HALO + t, lo:hi]
        acc = ug
        for back in range(1, w):
            acc = acc + ubuf[HALO - back:HALO - back + t, lo:hi]
        count = jnp.minimum(pos + 1, w).astype(F32)
        diff = (acc / count - ug).astype(_MXU_DTYPE)
        ycat[:, lo:hi] = (_mm(diff, pw_ref[g]) * ps_ref[:, lo:hi]).astype(ycat.dtype)

    def conv_block(rb, carry):
        base = rb * CONV_ROWS
        groups = CONV_ROWS // SUBLANES
        acc = jnp.broadcast_to(cb_ref[...].reshape(1, 1, cw_width), (groups, SUBLANES, cw_width))
        for k in range(CONV_KERNEL):
            a, r = divmod(CONV_KERNEL - 1 - k, SUBLANES)
            start = pl.multiple_of(base + (HALO - a * SUBLANES), SUBLANES)
            src = zbuf if r == 0 else zsh.at[r - 1]
            win = src[pl.ds(start, CONV_ROWS), :].reshape(groups, SUBLANES, cw_width)
            acc = acc + win * cw_ref[k][None]
        conv_scr[pl.ds(pl.multiple_of(base, CONV_ROWS), CONV_ROWS), :] = acc.reshape(CONV_ROWS, cw_width)
        return carry

    lax.fori_loop(0, t // CONV_ROWS, conv_block, 0)
    conv = conv_scr[...]
    mu = jnp.mean(conv, axis=-1, keepdims=True)
    cen = conv - mu
    var = jnp.mean(cen * cen, axis=-1, keepdims=True)
    zn = cen * lax.rsqrt(var + NORM_EPS) * lg_ref[...] + lb_ref[...]
    ycat[:, pw_width:pw_width + cw_width] = (zn * jax.nn.sigmoid(zn)).astype(ycat.dtype)

    o_ref[0] = x + _mm(ycat[...], wout_ref[...])


def _ab_mixer(x, g, win, pw, ps, cw, cb, lg, lb, wout):
    b, s, d = x.shape
    t = MIX_TOKENS
    pwid, cwid = ps.shape[1], cb.shape[1]
    return pl.pallas_call(
        _ab_kernel,
        grid=(b, s // t),
        in_specs=[
            pl.BlockSpec((1, t, d), lambda bi, j: (bi, j, 0)),
            _full((1, d)), _full(win.shape), _full(pw.shape), _full(ps.shape), _full(cw.shape),
            _full(cb.shape), _full(lg.shape), _full(lb.shape), _full(wout.shape),
        ],
        out_specs=pl.BlockSpec((1, t, d), lambda bi, j: (bi, j, 0)),
        out_shape=jax.ShapeDtypeStruct((b, s, d), F32),
        scratch_shapes=[
            pltpu.VMEM((HALO + t, pwid), F32),
            pltpu.VMEM((HALO + t, cwid), F32),
            pltpu.VMEM((SUBLANES - 1, HALO + t, cwid), F32),
            pltpu.VMEM((t, pwid + cwid), _MXU_DTYPE),
            pltpu.VMEM((t, cwid), F32),
        ],
        compiler_params=_params(),
        name="ab_mixer",
    )(x, g, win, pw, ps, cw, cb, lg, lb, wout)


def _cin_kernel(x_ref, g_ref, wn_ref, wt_ref, ww_ref, k_out, ki_out, qt_out, vt_out, qit_out, wt_out):
    t = x_ref.shape[1]
    aw = qt_out.shape[1]
    iw = qit_out.shape[1]
    for r0 in range(0, t, t // CIN_ROW_GROUPS):
        rows = slice(r0, r0 + t // CIN_ROW_GROUPS)
        h = _rms(x_ref[0, rows, :], g_ref[...]).astype(_MXU_DTYPE)
        nat = _mm(h, wn_ref[...])
        for p in range(aw // LANES):
            k_out[0, p, rows, :] = nat[:, p * LANES:(p + 1) * LANES].astype(k_out.dtype)
        ki_out[0, rows, :] = nat[:, aw:aw + IDX_DIM].astype(ki_out.dtype)
        tr = _mm_nt(wt_ref[...], h)
        qt_out[0, :, rows] = (tr[0:aw] * (HEAD_DIM ** -0.5 * LOG2E)).astype(qt_out.dtype)
        vt_out[0, :, rows] = tr[aw:2 * aw].astype(vt_out.dtype)
        qit_out[0, :, rows] = tr[2 * aw:2 * aw + iw].astype(qit_out.dtype)
        wt_out[0, :, rows] = _mm_nt(ww_ref[...], h)


def _c_inproj(x, g, wn, wt, ww):
    b, s, d = x.shape
    t = CIN_TOKENS
    aw = N_HEADS * HEAD_DIM
    iw = N_IDX_HEADS * IDX_DIM
    md = _MXU_DTYPE
    return pl.pallas_call(
        _cin_kernel,
        grid=(b, s // t),
        in_specs=[
            pl.BlockSpec((1, t, d), lambda bi, j: (bi, j, 0)),
            _full((1, d)), _full(wn.shape), _full(wt.shape), _full(ww.shape),
        ],
        out_specs=[
            pl.BlockSpec((1, aw // LANES, t, LANES), lambda bi, j: (bi, 0, j, 0)),
            pl.BlockSpec((1, t, IDX_DIM), lambda bi, j: (bi, j, 0)),
            pl.BlockSpec((1, aw, t), lambda bi, j: (bi, 0, j)),
            pl.BlockSpec((1, aw, t), lambda bi, j: (bi, 0, j)),
            pl.BlockSpec((1, iw, t), lambda bi, j: (bi, 0, j)),
            pl.BlockSpec((1, N_IDX_HEADS, t), lambda bi, j: (bi, 0, j)),
        ],
        out_shape=[
            jax.ShapeDtypeStruct((b, aw // LANES, s, LANES), md),
            jax.ShapeDtypeStruct((b, s, IDX_DIM), md),
            jax.ShapeDtypeStruct((b, aw, s), md),
            jax.ShapeDtypeStruct((b, aw, s), md),
            jax.ShapeDtypeStruct((b, iw, s), md),
            jax.ShapeDtypeStruct((b, N_IDX_HEADS, s), F32),
        ],
        compiler_params=_params(),
        name="c_inproj",
    )(x, g, wn, wt, ww)


def _bias_kernel(rel_ref, bkt_ref, o_ref):
    h = pl.program_id(1)
    bkt = bkt_ref[0]
    acc = jnp.zeros(bkt.shape, F32)
    for b in range(REL_BUCKETS):
        acc = jnp.where(bkt == b, rel_ref[b, h], acc)
    o_ref[0, 0] = acc * LOG2E


def _bias_tiles(rel_bias, bkt):
    nd = bkt.shape[0]
    return pl.pallas_call(
        _bias_kernel,
        grid=(nd, N_HEADS),
        in_specs=[
            pl.BlockSpec(memory_space=pltpu.SMEM),
            pl.BlockSpec((1, KC, TQ), lambda dl, h: (dl, 0, 0)),
        ],
        out_specs=pl.BlockSpec((1, 1, KC, TQ), lambda dl, h: (dl, h, 0, 0)),
        out_shape=jax.ShapeDtypeStruct((nd, N_HEADS, KC, TQ), F32),
        name="bias_tiles",
    )(rel_bias, bkt)


def _t5_bucket(dist):
    max_exact = REL_BUCKETS // 2
    d = jnp.maximum(dist, max_exact).astype(F32)
    large = max_exact + (jnp.log(d / max_exact) / math.log(REL_MAX_DIST / max_exact)
                         * (REL_BUCKETS - max_exact)).astype(I32)
    large = jnp.minimum(large, REL_BUCKETS - 1)
    return jnp.where(dist < max_exact, dist, large)


def _colsum8(v):
    return v.reshape(v.shape[0] // SUBLANES, SUBLANES, v.shape[1]).sum(axis=0)


def _colmax8(v):
    return v.reshape(v.shape[0] // SUBLANES, SUBLANES, v.shape[1]).max(axis=0)


def _dsa_block(n, far_ref, qt_ref, qit_ref, wt_ref, k_ref, ki_ref, vt_ref, tile_ref,
               key_scr, hi_scr, lo_scr, madd_scr, logit_scr, ot_scr, *, top_k):
    idx_scale = (N_IDX_HEADS * IDX_DIM) ** -0.5
    causal = lax.broadcasted_iota(I32, (KC, TQ), 0) <= lax.broadcasted_iota(I32, (KC, TQ), 1)
    rows = lambda c: slice(c * KC, (c + 1) * KC)

    for c in range(n):
        ki_c = ki_ref[0, rows(c), :]
        acc = jnp.zeros((KC, TQ), F32)
        for h in range(N_IDX_HEADS):
            r = _mm(ki_c, qit_ref[0, h * IDX_DIM:(h + 1) * IDX_DIM, :])
            acc = acc + jnp.maximum(r, 0.0) * wt_ref[0, h:h + 1, :]
        bits = lax.bitcast_convert_type(acc * idx_scale, I32)
        key = jnp.where(bits < 0, bits ^ 0x7FFFFFFF, bits)
        if c == n - 1:
            key = jnp.where(causal, key, INT_MIN)
        key_scr[rows(c), :] = key
        hi_scr[rows(c), :] = (key >> 16).astype(I16)

    if n * KC <= top_k:
        for c in range(n):
            madd_scr[rows(c), :] = jnp.where(key_scr[rows(c), :] != INT_MIN, 0.0, -jnp.inf)
    else:
        def count16(ref, pred, t16):
            parts = []
            for c in range(n):
                hit = pred(ref[rows(c), :], t16).astype(I16)
                parts += [hit[r * PACK16:(r + 1) * PACK16] for r in range(KC // PACK16)]
            while len(parts) > 1:
                parts = [a + b for a, b in zip(parts[0::2], parts[1::2])] + parts[len(parts) & ~1:]
            return parts[0].astype(I32).sum(axis=0, keepdims=True)

        def search16(ref, want):
            def bit_body(it, t_u):
                cand = t_u | lax.shift_left(jnp.int32(1), 15 - it)
                cnt = count16(ref, lambda v, t_: v >= t_, (cand - HALF16).astype(I16))
                return jnp.where(cnt >= want, cand, t_u)
            return lax.fori_loop(0, 16, bit_body, jnp.zeros((1, TQ), I32)) - HALF16

        t_hi = search16(hi_scr, top_k)
        above = count16(hi_scr, lambda v, t_: v > t_, t_hi.astype(I16))
        for c in range(n):
            k = key_scr[rows(c), :]
            lo = jnp.where((k >> 16) == t_hi, (k & 0xFFFF) - HALF16, -HALF16)
            lo_scr[rows(c), :] = lo.astype(I16)
        t_lo = search16(lo_scr, top_k - above)
        thr = lax.shift_left(t_hi, 16) | (t_lo + HALF16)

        above_t = above + count16(lo_scr, lambda v, t_: v > t_, t_lo.astype(I16))
        need = (top_k - above_t).astype(F32)
        tri = (lax.broadcasted_iota(I32, (KC, KC), 1)
               < lax.broadcasted_iota(I32, (KC, KC), 0)).astype(_MXU_DTYPE)
        seen = jnp.zeros((1, TQ), F32)
        for c in range(n):
            k = key_scr[rows(c), :]
            eq = (k == thr).astype(F32)
            before = _mm(tri, eq.astype(_MXU_DTYPE)) + seen
            sel = ((k > thr) | ((eq > 0.0) & (before < need))) & (k != INT_MIN)
            madd_scr[rows(c), :] = jnp.where(sel, 0.0, -jnp.inf)
            seen = seen + _colsum8(eq).sum(axis=0, keepdims=True)

    pair_row = lax.broadcasted_iota(I32, (2 * HEAD_DIM, TQ), 0)
    ones_rows = jnp.ones((ONES_ROWS, KC), _MXU_DTYPE)
    n_far = max(n - 2, 0)

    def scores(p, slot):
        prow = pl.ds(pl.multiple_of(p * 2 * HEAD_DIM, 2 * HEAD_DIM), 2 * HEAD_DIM)
        qt = qt_ref[0, prow, :]
        zero = jnp.zeros_like(qt)
        q2 = jnp.concatenate([jnp.where(pair_row < HEAD_DIM, qt, zero),
                              jnp.where(pair_row >= HEAD_DIM, qt, zero)], axis=1)
        fars = [far_ref[2 * p + half] * LOG2E for half in range(2)]
        m_far = [jnp.full((SUBLANES, TQ), -jnp.inf, F32)] * 2
        m_near = list(m_far)
        for c in range(n):
            s2 = _mm(k_ref[0, p, rows(c), :], q2)
            for half in range(2):
                s = s2[:, half * TQ:(half + 1) * TQ] + madd_scr[rows(c), :]
                if c < n_far:
                    m_far[half] = jnp.maximum(m_far[half], _colmax8(s))
                else:
                    s = s + tile_ref[n - 1 - c, 2 * p + half]
                    m_near[half] = jnp.maximum(m_near[half], _colmax8(s))
                logit_scr[2 * slot + half, rows(c), :] = s
        return [jnp.maximum(m_far[half] + fars[half], m_near[half]).max(axis=0, keepdims=True)
                for half in range(2)]

    def weighted_values(p, slot, m):
        prow = pl.ds(pl.multiple_of(p * 2 * HEAD_DIM, 2 * HEAD_DIM), 2 * HEAD_DIM)
        fars = [far_ref[2 * p + half] * LOG2E for half in range(2)]
        acc = jnp.zeros((2 * HEAD_DIM + ONES_ROWS, 2 * TQ), F32)
        for c in range(n):
            e2 = jnp.concatenate(
                [jnp.exp2(logit_scr[2 * slot + half, rows(c), :]
                          - (m[half] - fars[half] if c < n_far else m[half])).astype(_MXU_DTYPE)
                 for half in range(2)], axis=1)
            lhs = jnp.concatenate([vt_ref[0, prow, rows(c)], ones_rows], axis=0)
            acc = acc + _mm(lhs, e2)
        for half in range(2):
            cols = slice(half * TQ, (half + 1) * TQ)
            ot_scr[pl.ds(pl.multiple_of((2 * p + half) * HEAD_DIM, HEAD_DIM), HEAD_DIM), :] = (
                acc[half * HEAD_DIM:(half + 1) * HEAD_DIM, cols]
                / acc[2 * HEAD_DIM:2 * HEAD_DIM + 1, cols])

    def group_body(g, carry):
        ms = [scores(g * PAIRS_PER_STEP + u, u) for u in range(PAIRS_PER_STEP)]
        for u in range(PAIRS_PER_STEP):
            weighted_values(g * PAIRS_PER_STEP + u, u, ms[u])
        return carry

    lax.fori_loop(0, N_HEADS // 2 // PAIRS_PER_STEP, group_body, 0)


def _dsa_kernel(far_ref, x_ref, qt_ref, qit_ref, wt_ref, k_ref, ki_ref, vt_ref, tile_ref, wout_ref,
                o_ref, key_scr, hi_scr, lo_scr, madd_scr, logit_scr, ot_scr, *, top_k):
    i = pl.program_id(1)
    for n in range(1, k_ref.shape[2] // KC + 1):
        pl.when(i == n - 1)(functools.partial(
            _dsa_block, n, far_ref, qt_ref, qit_ref, wt_ref, k_ref, ki_ref, vt_ref, tile_ref,
            key_scr, hi_scr, lo_scr, madd_scr, logit_scr, ot_scr, top_k=top_k))
    o_ref[0] = x_ref[0] + _mm_tn(ot_scr[...].astype(_MXU_DTYPE), wout_ref[...])


def _dsa(x, far, qt, qit, wt, k, ki, vt, tiles, wout, top_k):
    b, s, d = x.shape
    aw = N_HEADS * HEAD_DIM
    iw = N_IDX_HEADS * IDX_DIM
    return pl.pallas_call(
        functools.partial(_dsa_kernel, top_k=top_k),
        grid=(b, s // TQ),
        in_specs=[
            pl.BlockSpec(memory_space=pltpu.SMEM),
            pl.BlockSpec((1, TQ, d), lambda bi, i: (bi, i, 0)),
            pl.BlockSpec((1, aw, TQ), lambda bi, i: (bi, 0, i)),
            pl.BlockSpec((1, iw, TQ), lambda bi, i: (bi, 0, i)),
            pl.BlockSpec((1, N_IDX_HEADS, TQ), lambda bi, i: (bi, 0, i)),
            pl.BlockSpec((1, aw // LANES, s, LANES), lambda bi, i: (bi, 0, 0, 0)),
            pl.BlockSpec((1, s, IDX_DIM), lambda bi, i: (bi, 0, 0)),
            pl.BlockSpec((1, aw, s), lambda bi, i: (bi, 0, 0)),
            _full(tiles.shape), _full(wout.shape),
        ],
        out_specs=pl.BlockSpec((1, TQ, d), lambda bi, i: (bi, i, 0)),
        out_shape=jax.ShapeDtypeStruct((b, s, d), F32),
        scratch_shapes=[
            pltpu.VMEM((s, TQ), I32),
            pltpu.VMEM((s, TQ), I16),
            pltpu.VMEM((s, TQ), I16),
            pltpu.VMEM((s, TQ), F32),
            pltpu.VMEM((2 * PAIRS_PER_STEP, s, TQ), F32),
            pltpu.VMEM((aw, TQ), F32),
        ],
        compiler_params=_params(),
        name="dsa",
    )(far, x, qt, qit, wt, k, ki, vt, tiles, wout)


def kernel(x, ffn1_norm, ffn1_w_gate, ffn1_w_up, ffn1_w_down, mix_norm, ffn2_norm, ffn2_w_gate,
           ffn2_w_up, ffn2_w_down, ab_w_in, pool_w, pool_scale, conv_w, conv_b, conv_ln_g,
           conv_ln_b, ab_w_out, c_w_in, c_w_out, rel_bias, final_norm):
    b, s, d = x.shape
    depth = ffn1_norm.shape[0]
    md = _MXU_DTYPE
    aw = N_HEADS * HEAD_DIM
    iw = N_IDX_HEADS * IDX_DIM
    top_k = min(TOPK_MAX, s // 4)
    assert s % MIX_TOKENS == 0 and s % CIN_TOKENS == 0 and (b * s) % FFN_TOKENS == 0 and s % TQ == 0 and top_k <= KC

    row = lambda v: v.reshape(1, -1)

    if depth > 1:
        sk = jnp.arange(KC, dtype=I32)[:, None]
        tq = jnp.arange(TQ, dtype=I32)[None, :]
        bkt = jnp.stack([_t5_bucket(jnp.maximum(tq + dl * TQ - sk, 0)) for dl in (0, 1)])
        tiles = _bias_tiles(rel_bias, bkt)
        assert 2 * TQ - KC + 1 >= REL_MAX_DIST
        far = rel_bias[REL_BUCKETS - 1]

    ffn1 = [_to_mxu(w_) for w_ in (ffn1_w_gate, ffn1_w_up, ffn1_w_down)]
    ffn2 = [_to_mxu(w_) for w_ in (ffn2_w_gate, ffn2_w_up, ffn2_w_down)]
    for layer in range(depth):
        i = layer // 2
        last = layer == depth - 1
        x2 = _ffn(x.reshape(b * s, d), row(ffn1_norm[layer]), *ffn1, row(final_norm), False, layer)
        x = x2.reshape(b, s, d)
        if layer % 2 == 0:
            x = _ab_mixer(x, row(mix_norm[layer]), ab_w_in[i].astype(md), pool_w[i].astype(md),
                          row(pool_scale[i]),
                          jnp.broadcast_to(conv_w[i][:, None, :], (CONV_KERNEL, SUBLANES, conv_w.shape[-1])),
                          row(conv_b[i]), row(conv_ln_g[i]),
                          row(conv_ln_b[i]), ab_w_out[i].astype(md))
        else:
            w = c_w_in[i]
            wq, wk, wv = w[:, 0:aw], w[:, aw:2 * aw], w[:, 2 * aw:3 * aw]
            wqi = w[:, 3 * aw:3 * aw + iw]
            wki = w[:, 3 * aw + iw:3 * aw + iw + IDX_DIM]
            www = w[:, 3 * aw + iw + IDX_DIM:]
            wn = jnp.concatenate([wk, wki], axis=1).astype(md)
            wt = jnp.concatenate([wq, wv, wqi], axis=1).T.astype(md)
            k, ki, qt, vt, qit, wts = _c_inproj(x, row(mix_norm[layer]), wn, wt, www.T.astype(md))
            x = _dsa(x, far, qt, qit, wts, k, ki, vt, tiles, c_w_out[i].astype(md), top_k)
        x2 = _ffn(x.reshape(b * s, d), row(ffn2_norm[layer]), *ffn2, row(final_norm), last, layer)
        x = x2.reshape(b, s, d)
    return x
```
